```python
import math
import jax, jax.numpy as jnp
from jax import lax
import numpy as np

D_MODEL = 2048
BATCH = 4
SEQ = 2048
DEPTH = 4

CHUNK = 64
D_MIX = D_MODEL
D_SSM = D_MIX // 2
SSM_GROUP = 16
N_SSM_GROUPS = D_SSM // SSM_GROUP
SSM_STATE = 64
N_DN_HEADS = 8
DN_HEAD_DIM = (D_MIX - D_SSM) // N_DN_HEADS
D_DN = N_DN_HEADS * DN_HEAD_DIM
CONV_WIDTH = 4
N_IN = D_SSM + 4 * D_DN + 2 * N_DN_HEADS
PEER_HEADS = 8
PEER_NKEYS = 128
PEER_EXPERTS = PEER_NKEYS * PEER_NKEYS
PEER_QDIM = 256
PEER_HALF = PEER_QDIM // 2
PEER_TOPK = 16
PEER_TOKEN_BLOCK = 128
DEEPNORM_ALPHA = (2.0 * DEPTH) ** 0.25
DEEPNORM_BETA = (8.0 * DEPTH) ** -0.25
LN_EPS = 1e-5
NORM_EPS = 1e-6

kernel_name = "hymba_s5_gdn_peer_deepnorm_trunk"


def layer_norm(x):
    xf = x.astype(jnp.float32)
    mu = xf.mean(-1, keepdims=True)
    var = jnp.square(xf - mu).mean(-1, keepdims=True)
    return ((xf - mu) * lax.rsqrt(var + LN_EPS)).astype(x.dtype)


def layer_norm_affine(x, gain, bias):
    xf = x.astype(jnp.float32)
    mu = xf.mean(-1, keepdims=True)
    var = jnp.square(xf - mu).mean(-1, keepdims=True)
    y = (xf - mu) * lax.rsqrt(var + LN_EPS) * gain.astype(jnp.float32) + bias.astype(jnp.float32)
    return y.astype(x.dtype)


def ada_modulate(x, shift, scale):
    return layer_norm(x) * (1.0 + scale) + shift


def l2norm(t):
    return t * lax.rsqrt(jnp.sum(t * t, axis=-1, keepdims=True) + NORM_EPS)


def s5_mixer(u, lam_re, lam_im, log_step, b_re, b_im, c_re, c_im, d_skip, w_glu):
    f32 = jnp.float32
    bsz, seq, _ = u.shape
    uf = u.astype(f32).reshape(bsz, seq, N_SSM_GROUPS, SSM_GROUP)
    lam = lax.complex(lam_re.astype(f32), lam_im.astype(f32))
    step = jnp.exp(log_step.astype(f32))[:, None]
    lam_bar = jnp.exp(lam * step)
    b = lax.complex(b_re.astype(f32), b_im.astype(f32))
    b_bar = ((lam_bar - 1.0) / lam)[..., None] * b
    bu = jnp.einsum('gpc,bsgc->bsgp', b_bar, uf.astype(jnp.complex64))
    a = jnp.broadcast_to(lam_bar, (1, seq, N_SSM_GROUPS, SSM_STATE))

    def combine(left, right):
        a_l, b_l = left
        a_r, b_r = right
        return a_l * a_r, a_r * b_l + b_r

    _, states = lax.associative_scan(combine, (a, bu), axis=1)
    cmat = lax.complex(c_re.astype(f32), c_im.astype(f32))
    y = jnp.einsum('gcp,bsgp->bsgc', cmat, states).real
    y = y + d_skip.astype(f32).reshape(N_SSM_GROUPS, SSM_GROUP) * uf
    y = jax.nn.gelu(y.reshape(bsz, seq, D_SSM))
    ab = y @ w_glu.astype(f32)
    out = ab[..., :D_SSM] * jax.nn.sigmoid(ab[..., D_SSM:])
    return out.astype(u.dtype)


def causal_depthwise_conv(x, w):
    ch = x.shape[-1]
    return lax.conv_general_dilated(
        x, w[:, None, :].astype(x.dtype), window_strides=(1,),
        padding=[(CONV_WIDTH - 1, 0)], dimension_numbers=('NWC', 'WIO', 'NWC'),
        feature_group_count=ch)


def gated_delta_chunked(q, k, v, g, beta):
    f32 = jnp.float32
    bsz, seq, nh, dk = q.shape
    dv = v.shape[-1]
    nc = seq // CHUNK
    q = q * (dk ** -0.5)

    def to_chunks(t):
        t = t.reshape(bsz, nc, CHUNK, nh, *t.shape[3:])
        return jnp.moveaxis(t, 3, 1)

    qc, kc, vc = to_chunks(q), to_chunks(k), to_chunks(v)
    gc, bc = to_chunks(g), to_chunks(beta)
    gcum = jnp.cumsum(gc, axis=-1)
    tril = jnp.tril(jnp.ones((CHUNK, CHUNK), dtype=bool))
    strict = jnp.tril(jnp.ones((CHUNK, CHUNK), dtype=bool), -1)
    diff = gcum[..., :, None] - gcum[..., None, :]
    decay = jnp.where(tril, jnp.exp(jnp.where(tril, diff, 0.0)), 0.0)
    kb = kc * bc[..., None]
    vb = vc * bc[..., None]
    lmat = jnp.where(strict, jnp.einsum('bhncd,bhnsd->bhncs', kb, kc) * decay, 0.0)
    eye = jnp.eye(CHUNK, dtype=f32)
    tmat = lax.linalg.triangular_solve(eye + lmat, jnp.broadcast_to(eye, lmat.shape),
                                       left_side=True, lower=True)
    w_val = tmat @ vb
    k_cum = tmat @ (kb * jnp.exp(gcum)[..., None])
    attn_intra = jnp.where(tril, jnp.einsum('bhncd,bhnsd->bhncs', qc, kc) * decay, 0.0)
    g_last = gcum[..., -1]
    k_tail = kc * jnp.exp(g_last[..., None] - gcum)[..., None]
    q_dec = qc * jnp.exp(gcum)[..., None]

    def step(state, inp):
        q_d, k_cd, w_v, a_in, k_t, gl = inp
        v_new = w_v - k_cd @ state
        out = q_d @ state + a_in @ v_new
        state = state * jnp.exp(gl)[..., None, None] + jnp.swapaxes(k_t, -1, -2) @ v_new
        return state, out

    xs = tuple(jnp.moveaxis(t, 2, 0) for t in (q_dec, k_cum, w_val, attn_intra, k_tail, g_last))
    s0 = jnp.zeros((bsz, nh, dk, dv), f32)
    _, out = lax.scan(step, s0, xs)
    return jnp.transpose(out, (1, 0, 3, 2, 4)).reshape(bsz, seq, nh, dv)


def deltanet_mixer(qkv, z, b_in, a_in, conv_w, a_log, dt_bias, norm_w):
    f32 = jnp.float32
    bsz, seq, _ = qkv.shape
    qkv = jax.nn.silu(causal_depthwise_conv(qkv, conv_w)).astype(f32)
    q, k, v = jnp.split(qkv, 3, axis=-1)
    q = l2norm(q.reshape(bsz, seq, N_DN_HEADS, DN_HEAD_DIM))
    k = l2norm(k.reshape(bsz, seq, N_DN_HEADS, DN_HEAD_DIM))
    v = v.reshape(bsz, seq, N_DN_HEADS, DN_HEAD_DIM)
    beta = jax.nn.sigmoid(b_in.astype(f32))
    g = -jnp.exp(a_log.astype(f32)) * jax.nn.softplus(a_in.astype(f32) + dt_bias.astype(f32))
    o = gated_delta_chunked(q, k, v, g, beta)
    o = o * lax.rsqrt(jnp.mean(o * o, axis=-1, keepdims=True) + NORM_EPS) * norm_w.astype(f32)
    o = o * jax.nn.silu(z.astype(f32).reshape(bsz, seq, N_DN_HEADS, DN_HEAD_DIM))
    return o.reshape(bsz, seq, D_DN).astype(z.dtype)


def peer_ffn(h, w_query, sub_keys, u_table, v_table):
    f32 = jnp.float32
    bsz, seq, d = h.shape
    q = (h @ w_query).astype(f32).reshape(bsz, seq, PEER_HEADS, 2, PEER_HALF)
    scores = jnp.einsum('bshid,hind->bshin', q, sub_keys.astype(f32))
    top_s, top_i = lax.top_k(scores, PEER_TOPK)
    cand_s = top_s[..., 0, :, None] + top_s[..., 1, None, :]
    cand_i = top_i[..., 0, :, None] * PEER_NKEYS + top_i[..., 1, None, :]
    cand_s = cand_s.reshape(bsz, seq, PEER_HEADS, PEER_TOPK * PEER_TOPK)
    cand_i = cand_i.reshape(bsz, seq, PEER_HEADS, PEER_TOPK * PEER_TOPK)
    best_s, best_pos = lax.top_k(cand_s, PEER_TOPK)
    expert_idx = jnp.take_along_axis(cand_i, best_pos, axis=-1)
    gates = jax.nn.softmax(best_s, axis=-1)
    nblk = (bsz * seq) // PEER_TOKEN_BLOCK
    hb = h.reshape(nblk, PEER_TOKEN_BLOCK, d)
    ib = expert_idx.reshape(nblk, PEER_TOKEN_BLOCK, PEER_HEADS * PEER_TOPK)
    gb = gates.reshape(nblk, PEER_TOKEN_BLOCK, PEER_HEADS * PEER_TOPK)

    def block(args):
        hx, idx, gt = args
        u_sel = u_table[idx]
        v_sel = v_table[idx]
        act = jax.nn.gelu(jnp.einsum('tkd,td->tk', u_sel, hx).astype(f32)) * gt
        return jnp.einsum('tk,tkd->td', act.astype(v_sel.dtype), v_sel)

    out = lax.map(block, (hb, ib, gb))
    return out.reshape(bsz, seq, d).astype(h.dtype)


def setup_inputs(seed: int = 0) -> dict:
    key = jax.random.key(seed)
    ks = jax.random.split(key, 32)
    f32 = jnp.float32
    L, G, P = DEPTH, N_SSM_GROUPS, SSM_STATE

    def nrm(k, shape, scale):
        return jax.random.normal(k, shape, f32) * scale

    dt = jnp.exp(jax.random.uniform(ks[16], (L, N_DN_HEADS), f32, math.log(1e-3), math.log(1e-1)))
    return {
        "x": nrm(ks[0], (BATCH, SEQ, D_MODEL), 1.0),
        "c": nrm(ks[1], (BATCH, D_MODEL), 1.0),
        "w_ada": nrm(ks[2], (L, D_MODEL, 6 * D_MODEL), 0.5 * D_MODEL ** -0.5),
        "b_ada": nrm(ks[3], (L, 6 * D_MODEL), 0.02),
        "w_in": nrm(ks[4], (L, D_MODEL, N_IN), D_MODEL ** -0.5),
        "ssm_lam_re": -0.5 + nrm(ks[5], (L, G, P), 0.01),
        "ssm_lam_im": math.pi * jnp.arange(P, dtype=f32) + nrm(ks[6], (L, G, P), 0.01),
        "ssm_log_step": jax.random.uniform(ks[7], (L, G), f32, math.log(1e-3), math.log(1e-1)),
        "ssm_b_re": nrm(ks[8], (L, G, P, SSM_GROUP), (2 * SSM_GROUP) ** -0.5),
        "ssm_b_im": nrm(ks[9], (L, G, P, SSM_GROUP), (2 * SSM_GROUP) ** -0.5),
        "ssm_c_re": nrm(ks[10], (L, G, SSM_GROUP, P), (2 * P) ** -0.5),
        "ssm_c_im": nrm(ks[11], (L, G, SSM_GROUP, P), (2 * P) ** -0.5),
        "ssm_d": nrm(ks[12], (L, D_SSM), 1.0),
        "ssm_w_glu": nrm(ks[13], (L, D_SSM, 2 * D_SSM), D_SSM ** -0.5),
        "dn_conv_w": nrm(ks[14], (L, CONV_WIDTH, 3 * D_DN), CONV_WIDTH ** -0.5),
        "dn_a_log": jnp.log(jax.random.uniform(ks[15], (L, N_DN_HEADS), f32, 1.0, 16.0)),
        "dn_dt_bias": dt + jnp.log(-jnp.expm1(-dt)),
        "dn_norm_w": 1.0 + nrm(ks[17], (L, DN_HEAD_DIM), 0.02),
        "w_out": nrm(ks[18], (L, D_MIX, D_MODEL), D_MIX ** -0.5 * DEEPNORM_BETA),
        "ln1_g": 1.0 + nrm(ks[19], (L, D_MODEL), 0.02),
        "ln1_b": nrm(ks[20], (L, D_MODEL), 0.02),
        "peer_w_query": nrm(ks[21], (L, D_MODEL, PEER_HEADS * PEER_QDIM), D_MODEL ** -0.5),
        "peer_sub_keys": nrm(ks[22], (L, PEER_HEADS, 2, PEER_NKEYS, PEER_HALF), PEER_HALF ** -0.5),
        "peer_u": nrm(ks[23], (L, PEER_EXPERTS, D_MODEL), D_MODEL ** -0.5),
        "peer_v": nrm(ks[24], (L, PEER_EXPERTS, D_MODEL), PEER_HEADS ** -0.5 * DEEPNORM_BETA),
        "ln2_g": 1.0 + nrm(ks[25], (L, D_MODEL), 0.02),
        "ln2_b": nrm(ks[26], (L, D_MODEL), 0.02),
    }


def reference(x, c, w_ada, b_ada, w_in, ssm_lam_re, ssm_lam_im, ssm_log_step, ssm_b_re,
              ssm_b_im, ssm_c_re, ssm_c_im, ssm_d, ssm_w_glu, dn_conv_w, dn_a_log,
              dn_dt_bias, dn_norm_w, w_out, ln1_g, ln1_b, peer_w_query, peer_sub_keys,
              peer_u, peer_v, ln2_g, ln2_b):
    split_at = [D_SSM, D_SSM + 3 * D_DN, D_SSM + 4 * D_DN, D_SSM + 4 * D_DN + N_DN_HEADS]
    c_act = jax.nn.silu(c)
    for l in range(DEPTH):
        mod = c_act @ w_ada[l] + b_ada[l]
        sh1, sc1, gt1, sh2, sc2, gt2 = jnp.split(mod[:, None, :], 6, axis=-1)

        hmix = ada_modulate(x, sh1, sc1)
        proj = hmix @ w_in[l]
        u_ssm, qkv, z, b_in, a_in = jnp.split(proj, split_at, axis=-1)
        y_ssm = s5_mixer(u_ssm, ssm_lam_re[l], ssm_lam_im[l], ssm_log_step[l], ssm_b_re[l],
                         ssm_b_im[l], ssm_c_re[l], ssm_c_im[l], ssm_d[l], ssm_w_glu[l])
        y_dn = deltanet_mixer(qkv, z, b_in, a_in, dn_conv_w[l], dn_a_log[l], dn_dt_bias[l],
                              dn_norm_w[l])
        y = jnp.concatenate([y_ssm.astype(x.dtype), y_dn.astype(x.dtype)], axis=-1) @ w_out[l]
        x = layer_norm_affine(DEEPNORM_ALPHA * x + gt1 * y, ln1_g[l], ln1_b[l])

        hffn = ada_modulate(x, sh2, sc2)
        y = peer_ffn(hffn, peer_w_query[l], peer_sub_keys[l], peer_u[l], peer_v[l])
        x = layer_norm_affine(DEEPNORM_ALPHA * x + gt2 * y, ln2_g[l], ln2_b[l])
    return x
```

```python
import functools
import math

import jax
import jax.numpy as jnp
from jax import lax
from jax.experimental import pallas as pl
from jax.experimental.pallas import tpu as pltpu

F32 = jnp.float32
BF16 = jnp.bfloat16
HI = lax.Precision.HIGHEST

D_MODEL = 2048
D_SSM = 1024
SSM_GROUP = 16
N_GROUPS = D_SSM // SSM_GROUP
SSM_STATE = 64
SSM_CHUNK = 16
N_HEADS = 8
HEAD_DIM = 128
D_DN = N_HEADS * HEAD_DIM
DN_CHUNK = 64
CONV_WIDTH = 4
N_MAIN = D_SSM + 4 * D_DN
PEER_HEADS = 8
PEER_NKEYS = 128
PEER_EXPERTS = PEER_NKEYS * PEER_NKEYS
PEER_TOPK = 16
PEER_SUB = 8
PEER_TE = PEER_SUB * PEER_NKEYS
LN_EPS = 1e-5
NORM_EPS = 1e-6

LANES = 128
VMEM_LIMIT = 56 * 1024 * 1024


def _params(*sem):
    return pltpu.CompilerParams(dimension_semantics=sem, vmem_limit_bytes=VMEM_LIMIT)


def _gelu(x):
    return 0.5 * x * (1.0 + jnp.tanh(math.sqrt(2.0 / math.pi) * (x + 0.044715 * (x * x * x))))


def _sigmoid(x):
    return 1.0 / (1.0 + jnp.exp(-x))


def _dot(a, b, precision=None):
    return jnp.dot(a, b, precision=precision, preferred_element_type=F32)


def _dot_nt(a, b, precision=None):
    return lax.dot_general(a, b, (((1,), (1,)), ((), ())), precision=precision,
                           preferred_element_type=F32)


def _dot_tn(a, b, precision=None):
    return lax.dot_general(a, b, (((0,), (0,)), ((), ())), precision=precision,
                           preferred_element_type=F32)


def _ada_kernel(c_ref, w_ref, b_ref, o_ref):
    c = c_ref[...]
    ca = c * _sigmoid(c)
    hi = ca.astype(BF16)
    lo = (ca - hi.astype(F32)).astype(BF16)
    w = w_ref[0].astype(BF16)
    o_ref[0] = _dot(hi, w) + _dot(lo, w) + b_ref[0]


def _ada_mod(c, w_ada, b_ada):
    depth, d, n = w_ada.shape
    tn = 1024
    cp = jnp.zeros((8, d), F32).at[: c.shape[0]].set(c)
    return pl.pallas_call(
        _ada_kernel,
        grid=(depth, n // tn),
        in_specs=[pl.BlockSpec((8, d), lambda l, j: (0, 0)),
                  pl.BlockSpec((1, d, tn), lambda l, j: (l, 0, j)),
                  pl.BlockSpec((1, 1, tn), lambda l, j: (l, 0, j))],
        out_specs=pl.BlockSpec((1, 8, tn), lambda l, j: (l, 0, j)),
        out_shape=jax.ShapeDtypeStruct((depth, 8, n), F32),
        compiler_params=_params("parallel", "parallel"),
        name="ada_mod",
    )(cp, w_ada, b_ada.reshape(depth, 1, n))


def _ln(x):
    mu = jnp.mean(x, axis=-1, keepdims=True)
    xc = x - mu
    var = jnp.mean(xc * xc, axis=-1, keepdims=True)
    return xc * lax.rsqrt(var + LN_EPS)


def _ln_mod_kernel(x_ref, sh_ref, sc_ref, o_ref):
    o_ref[...] = (_ln(x_ref[...]) * (1.0 + sc_ref[0]) + sh_ref[0]).astype(o_ref.dtype)


def _ln_mod(x, sh, sc, seq):
    t, d = x.shape
    tm = min(256, seq)
    per = seq // tm
    return pl.pallas_call(
        _ln_mod_kernel,
        grid=(t // tm,),
        in_specs=[pl.BlockSpec((tm, d), lambda i: (i, 0)),
                  pl.BlockSpec((1, 1, d), lambda i: (i // per, 0, 0)),
                  pl.BlockSpec((1, 1, d), lambda i: (i // per, 0, 0))],
        out_specs=pl.BlockSpec((tm, d), lambda i: (i, 0)),
        out_shape=jax.ShapeDtypeStruct((t, d), BF16),
        compiler_params=_params("parallel"),
        name="ln_mod",
    )(x, sh, sc)


def _res_ln_kernel(x_ref, y_ref, gt_ref, g_ref, b_ref, o_ref, *, alpha):
    r = alpha * x_ref[...] + gt_ref[0] * y_ref[...]
    o_ref[...] = _ln(r) * g_ref[...] + b_ref[...]


def _res_ln(x, y, gt, gain, bias, seq, alpha):
    t, d = x.shape
    tm = min(256, seq)
    per = seq // tm
    return pl.pallas_call(
        functools.partial(_res_ln_kernel, alpha=alpha),
        grid=(t // tm,),
        in_specs=[pl.BlockSpec((tm, d), lambda i: (i, 0)),
                  pl.BlockSpec((tm, d), lambda i: (i, 0)),
                  pl.BlockSpec((1, 1, d), lambda i: (i // per, 0, 0)),
                  pl.BlockSpec((1, d), lambda i: (0, 0)),
                  pl.BlockSpec((1, d), lambda i: (0, 0))],
        out_specs=pl.BlockSpec((tm, d), lambda i: (i, 0)),
        out_shape=jax.ShapeDtypeStruct((t, d), F32),
        compiler_params=_params("parallel"),
        name="res_ln",
    )(x, y, gt, gain.reshape(1, d), bias.reshape(1, d))


def _mm_kernel(a_ref, b_ref, o_ref):
    o_ref[...] = _dot(a_ref[...], b_ref[...]).astype(o_ref.dtype)


def _mm(a, b, out_dtype, tm, tn, name):
    m, k = a.shape
    n = b.shape[1]
    tm, tn = min(tm, m), min(tn, n)
    return pl.pallas_call(
        _mm_kernel,
        grid=(m // tm, n // tn),
        in_specs=[pl.BlockSpec((tm, k), lambda i, j: (i, 0)),
                  pl.BlockSpec((k, tn), lambda i, j: (0, j))],
        out_specs=pl.BlockSpec((tm, tn), lambda i, j: (i, j)),
        out_shape=jax.ShapeDtypeStruct((m, n), out_dtype),
        compiler_params=_params("parallel", "parallel"),
        name=name,
    )(a, b)


def _s5_prep(lam_re, lam_im, log_step, b_re, b_im, c_re, c_im, d_skip):
    g, p, nch = N_GROUPS, SSM_STATE, SSM_CHUNK
    step = jnp.exp(log_step)[:, None]
    zr, zi = lam_re * step, lam_im * step
    ks = jnp.arange(nch + 1, dtype=F32)[:, None, None]
    mag = jnp.exp(ks * zr)
    pr, pi = mag * jnp.cos(ks * zi), mag * jnp.sin(ks * zi)
    nr, ni = pr[1] - 1.0, pi[1]
    den = lam_re * lam_re + lam_im * lam_im
    fr = (nr * lam_re + ni * lam_im) / den
    fi = (ni * lam_re - nr * lam_im) / den
    bbr = fr[..., None] * b_re - fi[..., None] * b_im
    bbi = fr[..., None] * b_im + fi[..., None] * b_re
    er = pr[:nch, :, :, None] * bbr - pi[:nch, :, :, None] * bbi
    ei = pr[:nch, :, :, None] * bbi + pi[:nch, :, :, None] * bbr
    kk = (jnp.einsum('gap,kgpc->kgac', c_re, er, precision=HI)
          - jnp.einsum('gap,kgpc->kgac', c_im, ei, precision=HI))
    dlt = jnp.arange(nch)[None, :] - jnp.arange(nch)[:, None]
    m = jnp.where((dlt >= 0)[:, :, None, None, None], kk[jnp.clip(dlt, 0, nch - 1)], 0.0)
    m_intra = m.transpose(2, 0, 4, 1, 3).reshape(g, nch * SSM_GROUP, nch * SSM_GROUP)
    m_state = jnp.concatenate([er[::-1].transpose(1, 0, 3, 2), ei[::-1].transpose(1, 0, 3, 2)],
                              axis=-1).reshape(g, nch * SSM_GROUP, 2 * p)
    wr = c_re[None] * pr[1:, :, None, :] - c_im[None] * pi[1:, :, None, :]
    wi = c_re[None] * pi[1:, :, None, :] + c_im[None] * pr[1:, :, None, :]
    m_out = jnp.concatenate([wr.transpose(1, 3, 0, 2), -wi.transpose(1, 3, 0, 2)],
                            axis=1).reshape(g, 2 * p, nch * SSM_GROUP)
    a_re, a_im = pr[nch].reshape(1, g * p), pi[nch].reshape(1, g * p)
    dvec = jnp.tile(d_skip.reshape(g, 1, SSM_GROUP), (1, nch, 1)).reshape(g, 1, nch * SSM_GROUP)
    return m_state, m_intra, m_out, a_re, a_im, dvec


def _s5_state_kernel(x_ref, m_ref, o_ref):
    o_ref[0] = _dot(x_ref[0], m_ref[0], HI)


def _s5_scan_kernel(s_ref, ar_ref, ai_ref, h_ref):
    nchunks = s_ref.shape[0]
    ar, ai = ar_ref[...], ai_ref[...]

    def body(c, carry):
        hr, hi = carry
        h_ref[c, 0] = hr
        h_ref[c, 1] = hi
        sr, si = s_ref[c, 0], s_ref[c, 1]
        return ar * hr - ai * hi + sr, ar * hi + ai * hr + si

    zero = jnp.zeros(s_ref.shape[2:], F32)
    lax.fori_loop(0, nchunks, body, (zero, zero))


def _s5_out_kernel(x_ref, h_ref, mi_ref, mo_ref, d_ref, o_ref):
    x = x_ref[0]
    y = _dot(x, mi_ref[0], HI) + _dot(h_ref[0], mo_ref[0], HI) + d_ref[0] * x
    o_ref[0] = _gelu(y).astype(o_ref.dtype)


def _glu_kernel(y_ref, wa_ref, wb_ref, o_ref):
    y = y_ref[...]
    o_ref[...] = (_dot(y, wa_ref[...]) * _sigmoid(_dot(y, wb_ref[...]))).astype(o_ref.dtype)


def _s5_mixer(proj, bsz, seq, prep, w_glu):
    g, p, nch = N_GROUPS, SSM_STATE, SSM_CHUNK
    m_state, m_intra, m_out, a_re, a_im, dvec = prep
    ncs = seq // nch
    rows = ncs * bsz
    width = nch * SSM_GROUP
    xg = proj[:, :D_SSM].reshape(bsz, ncs, nch, g, SSM_GROUP).transpose(3, 1, 0, 2, 4)
    xg = xg.reshape(g, rows, width)

    s_loc = pl.pallas_call(
        _s5_state_kernel,
        grid=(g,),
        in_specs=[pl.BlockSpec((1, rows, width), lambda i: (i, 0, 0)),
                  pl.BlockSpec((1, width, 2 * p), lambda i: (i, 0, 0))],
        out_specs=pl.BlockSpec((1, rows, 2 * p), lambda i: (i, 0, 0)),
        out_shape=jax.ShapeDtypeStruct((g, rows, 2 * p), F32),
        compiler_params=_params("parallel"),
        name="s5_state",
    )(xg, m_state)

    s_t = s_loc.reshape(g, ncs, bsz, 2, p).transpose(1, 3, 2, 0, 4).reshape(ncs, 2, bsz, g * p)
    lb = 1024
    h_t = pl.pallas_call(
        _s5_scan_kernel,
        grid=(g * p // lb,),
        in_specs=[pl.BlockSpec((ncs, 2, bsz, lb), lambda i: (0, 0, 0, i)),
                  pl.BlockSpec((1, lb), lambda i: (0, i)),
                  pl.BlockSpec((1, lb), lambda i: (0, i))],
        out_specs=pl.BlockSpec((ncs, 2, bsz, lb), lambda i: (0, 0, 0, i)),
        out_shape=jax.ShapeDtypeStruct((ncs, 2, bsz, g * p), F32),
        compiler_params=_params("parallel"),
        name="s5_scan",
    )(s_t, a_re, a_im)
    h_prev = h_t.reshape(ncs, 2, bsz, g, p).transpose(3, 0, 2, 1, 4).reshape(g, rows, 2 * p)

    yg = pl.pallas_call(
        _s5_out_kernel,
        grid=(g,),
        in_specs=[pl.BlockSpec((1, rows, width), lambda i: (i, 0, 0)),
                  pl.BlockSpec((1, rows, 2 * p), lambda i: (i, 0, 0)),
                  pl.BlockSpec((1, width, width), lambda i: (i, 0, 0)),
                  pl.BlockSpec((1, 2 * p, width), lambda i: (i, 0, 0)),
                  pl.BlockSpec((1, 1, width), lambda i: (i, 0, 0))],
        out_specs=pl.BlockSpec((1, rows, width), lambda i: (i, 0, 0)),
        out_shape=jax.ShapeDtypeStruct((g, rows, width), BF16),
        compiler_params=_params("parallel"),
        name="s5_out",
    )(xg, h_prev, m_intra, m_out, dvec)
    y = yg.reshape(g, ncs, bsz, nch, SSM_GROUP).transpose(2, 1, 3, 0, 4).reshape(bsz * seq, D_SSM)

    t = bsz * seq
    tm, tn = min(512, t), 512
    nj = D_SSM // tn
    return pl.pallas_call(
        _glu_kernel,
        grid=(t // tm, nj),
        in_specs=[pl.BlockSpec((tm, D_SSM), lambda i, j: (i, 0)),
                  pl.BlockSpec((D_SSM, tn), lambda i, j: (0, j)),
                  pl.BlockSpec((D_SSM, tn), lambda i, j: (0, j + nj))],
        out_specs=pl.BlockSpec((tm, tn), lambda i, j: (i, j)),
        out_shape=jax.ShapeDtypeStruct((t, D_SSM), BF16),
        compiler_params=_params("parallel", "parallel"),
        name="s5_glu",
    )(y, w_glu, w_glu)


def _dn_conv_kernel(x_ref, w_ref, o_ref):
    j = pl.program_id(1)
    x = x_ref[0]
    w = w_ref[...]
    row = lax.broadcasted_iota(jnp.int32, x.shape, 0)
    acc = x * w[CONV_WIDTH - 1:CONV_WIDTH, :]
    for k in range(CONV_WIDTH - 1):
        sh = CONV_WIDTH - 1 - k
        xs = jnp.where(row >= sh, pltpu.roll(x, sh, axis=0), 0.0)
        acc = acc + xs * w[k:k + 1, :]
    y = acc * _sigmoid(acc)
    nrm = y * lax.rsqrt(jnp.sum(y * y, axis=-1, keepdims=True) + NORM_EPS)
    qscale = jnp.where(j < N_HEADS, HEAD_DIM ** -0.5, 1.0).astype(F32)
    o_ref[0] = jnp.where(j < 2 * N_HEADS, nrm * qscale, y)


def _gates_kernel(x_ref, alog_ref, dtb_ref, o_ref):
    x = x_ref[...]
    lane = lax.broadcasted_iota(jnp.int32, x.shape, 1)
    xs = x + dtb_ref[...]
    softplus = jnp.maximum(xs, 0.0) + jnp.log1p(jnp.exp(-jnp.abs(xs)))
    o_ref[...] = jnp.where(lane < N_HEADS, _sigmoid(x), -jnp.exp(alog_ref[...]) * softplus)


def _dn_kernel(q_ref, k_ref, v_ref, z_ref, gt_ref, nw_ref, o_ref,
               wv_s, kc_s, at_s, kt_s, qd_s, eg_s, *, nchunks):
    h = pl.program_id(1)
    c, dh = DN_CHUNK, HEAD_DIM
    li = lax.broadcasted_iota(jnp.int32, (LANES, LANES), 0)
    sel_beta = (li == h).astype(F32)
    sel_g = (li == h + N_HEADS).astype(F32)
    ri = lax.broadcasted_iota(jnp.int32, (c, c), 0)
    ci = lax.broadcasted_iota(jnp.int32, (c, c), 1)
    tril, strict = ri >= ci, ri > ci
    tril_f = tril.astype(F32)
    triu_f = (ri <= ci).astype(F32)
    eye = (ri == ci).astype(F32)
    ones_c = jnp.ones((c, c), F32)
    ones_d = jnp.ones((dh, c), F32)

    def prepare(n, carry):
        rows = pl.ds(pl.multiple_of(n * c, c), c)
        gt = gt_ref[0, rows, :]
        beta = _dot(gt, sel_beta, HI)
        gl = _dot(gt, sel_g, HI)
        gcum = _dot(tril_f, gl, HI)
        gcum_row = _dot(ones_c, gl[:, :c] * triu_f, HI)
        glast = _dot(ones_d, gl, HI)
        diff = gcum[:, :c] - gcum_row
        decay = jnp.where(tril, jnp.exp(jnp.where(tril, diff, 0.0)), 0.0)
        q, k, v = q_ref[0, rows, :], k_ref[0, rows, :], v_ref[0, rows, :]
        kb, vb = k * beta, v * beta
        lmat = jnp.where(strict, _dot_nt(kb, k, HI) * decay, 0.0)
        tmat, lpow = eye - lmat, lmat
        for _ in range(5):
            lpow = _dot(lpow, lpow, HI)
            tmat = tmat + _dot(tmat, lpow, HI)
        egc = jnp.exp(gcum)
        wv_s[rows, :] = _dot(tmat, vb, HI)
        kc_s[rows, :] = _dot(tmat, kb * egc, HI)
        at_s[rows, :] = jnp.where(tril, _dot_nt(q, k, HI) * decay, 0.0)
        kt_s[rows, :] = k * jnp.exp(glast[:c, :] - gcum)
        qd_s[rows, :] = q * egc
        eg_s[pl.ds(pl.multiple_of(n * dh, dh), dh), :] = jnp.exp(glast)
        return carry

    lax.fori_loop(0, nchunks, prepare, 0)

    def recur(n, state):
        rows = pl.ds(pl.multiple_of(n * c, c), c)
        v_new = wv_s[rows, :] - _dot(kc_s[rows, :], state, HI)
        out = _dot(qd_s[rows, :], state, HI) + _dot(at_s[rows, :], v_new, HI)
        eg = eg_s[pl.ds(pl.multiple_of(n * dh, dh), dh), :]
        state = state * eg + _dot_tn(kt_s[rows, :], v_new, HI)
        o = out * lax.rsqrt(jnp.mean(out * out, axis=-1, keepdims=True) + NORM_EPS) * nw_ref[...]
        z = z_ref[0, rows, :]
        o_ref[0, rows, :] = (o * (z * _sigmoid(z))).astype(o_ref.dtype)
        return state

    lax.fori_loop(0, nchunks, recur, jnp.zeros((dh, dh), F32))


def _dn_mixer(proj, gates_raw, bsz, seq, conv_w, a_log, dt_bias, norm_w):
    t = bsz * seq
    proj3 = proj.reshape(bsz, seq, N_MAIN)
    qkv_off = D_SSM // LANES
    z_off = (D_SSM + 3 * D_DN) // LANES
    n_tiles = 3 * D_DN // LANES
    qkv = pl.pallas_call(
        _dn_conv_kernel,
        grid=(bsz, n_tiles),
        in_specs=[pl.BlockSpec((1, seq, LANES), lambda b, j: (b, 0, j + qkv_off)),
                  pl.BlockSpec((CONV_WIDTH, LANES), lambda b, j: (0, j))],
        out_specs=pl.BlockSpec((1, seq, LANES), lambda b, j: (b, 0, j)),
        out_shape=jax.ShapeDtypeStruct((bsz, seq, 3 * D_DN), F32),
        compiler_params=_params("parallel", "parallel"),
        name="dn_conv",
    )(proj3, conv_w)

    pad = jnp.zeros((1, LANES), F32)
    alog_p = pad.at[0, N_HEADS:2 * N_HEADS].set(a_log)
    dtb_p = pad.at[0, N_HEADS:2 * N_HEADS].set(dt_bias)
    tm = min(1024, t)
    gates = pl.pallas_call(
        _gates_kernel,
        grid=(t // tm,),
        in_specs=[pl.BlockSpec((tm, LANES), lambda i: (i, 0)),
                  pl.BlockSpec((1, LANES), lambda i: (0, 0)),
                  pl.BlockSpec((1, LANES), lambda i: (0, 0))],
        out_specs=pl.BlockSpec((tm, LANES), lambda i: (i, 0)),
        out_shape=jax.ShapeDtypeStruct((t, LANES), F32),
        compiler_params=_params("parallel"),
        name="dn_gates",
    )(gates_raw, alog_p, dtb_p).reshape(bsz, seq, LANES)

    nchunks = seq // DN_CHUNK
    head = lambda off: pl.BlockSpec((1, seq, LANES), lambda b, h: (b, 0, h + off))
    return pl.pallas_call(
        functools.partial(_dn_kernel, nchunks=nchunks),
        grid=(bsz, N_HEADS),
        in_specs=[head(0), head(N_HEADS), head(2 * N_HEADS),
                  pl.BlockSpec((1, seq, LANES), lambda b, h: (b, 0, h + z_off)),
                  pl.BlockSpec((1, seq, LANES), lambda b, h: (b, 0, 0)),
                  pl.BlockSpec((1, HEAD_DIM), lambda b, h: (0, 0))],
        out_specs=pl.BlockSpec((1, seq, LANES), lambda b, h: (b, 0, h)),
        out_shape=jax.ShapeDtypeStruct((bsz, seq, D_DN), BF16),
        scratch_shapes=[pltpu.VMEM((seq, HEAD_DIM), F32),
                        pltpu.VMEM((seq, HEAD_DIM), F32),
                        pltpu.VMEM((seq, DN_CHUNK), F32),
                        pltpu.VMEM((seq, HEAD_DIM), F32),
                        pltpu.VMEM((seq, HEAD_DIM), F32),
                        pltpu.VMEM((nchunks * HEAD_DIM, HEAD_DIM), F32)],
        compiler_params=_params("parallel", "parallel"),
        name="dn_chunk",
    )(qkv, qkv, qkv, proj3, gates, norm_w.reshape(1, HEAD_DIM)).reshape(t, D_DN)


def _out_ln_kernel(a1_ref, a2_ref, w1_ref, w2_ref, x_ref, gt_ref, g_ref, b_ref, o_ref, *, alpha):
    y = _dot(a1_ref[...], w1_ref[...]) + _dot(a2_ref[...], w2_ref[...])
    r = alpha * x_ref[...] + gt_ref[0] * y
    o_ref[...] = _ln(r) * g_ref[...] + b_ref[...]


def _out_ln(y_ssm, y_dn, w_out, x, gt, gain, bias, seq, alpha):
    t, d = x.shape
    tm = min(256, seq)
    per = seq // tm
    return pl.pallas_call(
        functools.partial(_out_ln_kernel, alpha=alpha),
        grid=(t // tm,),
        in_specs=[pl.BlockSpec((tm, D_SSM), lambda i: (i, 0)),
                  pl.BlockSpec((tm, D_DN), lambda i: (i, 0)),
                  pl.BlockSpec((D_SSM, d), lambda i: (0, 0)),
                  pl.BlockSpec((D_DN, d), lambda i: (1, 0)),
                  pl.BlockSpec((tm, d), lambda i: (i, 0)),
                  pl.BlockSpec((1, 1, d), lambda i: (i // per, 0, 0)),
                  pl.BlockSpec((1, d), lambda i: (0, 0)),
                  pl.BlockSpec((1, d), lambda i: (0, 0))],
        out_specs=pl.BlockSpec((tm, d), lambda i: (i, 0)),
        out_shape=jax.ShapeDtypeStruct((t, d), F32),
        compiler_params=_params("parallel"),
        name="out_ln",
    )(y_ssm, y_dn, w_out, w_out, x, gt, gain.reshape(1, d), bias.reshape(1, d))


def _top_values(s, count):
    vals = []
    for r in range(count):
        m = jnp.max(s, axis=0, keepdims=True)
        vals.append(m)
        if r + 1 < count:
            s = jnp.where(s == m, -jnp.inf, s)
    return vals


def _peer_route_kernel(ht_ref, wq_ref, keys_ref, a_ref, s0_ref, b_ref, s1_ref, thr_ref, q_s):
    q_s[...] = _dot(wq_ref[...], ht_ref[...])
    k = PEER_TOPK

    def head(h, carry):
        base = pl.multiple_of(h * 2 * PEER_NKEYS, 2 * PEER_NKEYS)
        s0 = _dot(keys_ref[h, 0], q_s[pl.ds(base, PEER_NKEYS), :], HI)
        s1 = _dot(keys_ref[h, 1], q_s[pl.ds(base + PEER_NKEYS, PEER_NKEYS), :], HI)
        v0, v1 = _top_values(s0, k), _top_values(s1, k)
        v1m = jnp.concatenate(v1, axis=0)
        cands = [v0[i] + v1m for i in range(k)]
        top = v0[0] + v1[0]
        z = jnp.zeros_like(top)
        thr = top
        for r in range(k):
            m = cands[0]
            for cnd in cands[1:]:
                m = jnp.maximum(m, cnd)
            m = jnp.max(m, axis=0, keepdims=True)
            z = z + jnp.exp(m - top)
            thr = m
            if r + 1 < k:
                cands = [jnp.where(cnd == m, -jnp.inf, cnd) for cnd in cands]
        a_ref[h] = jnp.exp(s0 - v0[0]) / z
        b_ref[h] = jnp.exp(s1 - v1[0])
        s0_ref[h] = s0
        s1_ref[h] = s1
        thr_ref[pl.ds(h, 1), :] = thr
        return carry

    lax.fori_loop(0, PEER_HEADS, head, 0)


def _peer_dense_kernel(ht_ref, u_ref, vt_ref, a_ref, s0_ref, b_ref, s1_ref, thr_ref, o_ref,
                       act_s, p_s, *, tl):
    e = pl.program_id(1)

    @pl.when(e == 0)
    def _():
        o_ref[...] = jnp.zeros_like(o_ref)

    act_s[...] = _gelu(_dot(u_ref[...], ht_ref[...]))
    first = pl.ds(pl.multiple_of(e * PEER_SUB, PEER_SUB), PEER_SUB)
    for lc in range(tl // LANES):
        ls = slice(lc * LANES, (lc + 1) * LANES)
        s0_rows = [s0_ref[h, first, ls] for h in range(PEER_HEADS)]
        a_rows = [a_ref[h, first, ls] for h in range(PEER_HEADS)]
        for ii in range(PEER_SUB):
            er = slice(ii * PEER_NKEYS, (ii + 1) * PEER_NKEYS)
            w = jnp.zeros((PEER_NKEYS, LANES), F32)
            for h in range(PEER_HEADS):
                ssum = s0_rows[h][ii:ii + 1, :] + s1_ref[h, :, ls]
                keep = ssum >= thr_ref[h:h + 1, ls]
                w = w + jnp.where(keep, b_ref[h, :, ls], 0.0) * a_rows[h][ii:ii + 1, :]
            p_s[er, ls] = (act_s[er, ls] * w).astype(BF16)
    o_ref[...] += _dot(vt_ref[...], p_s[...])


def _peer_ffn(h_t, wq_t, keys, u_bf, vt_bf):
    d, t = h_t.shape
    nq = wq_t.shape[0]
    tl = min(256, t)
    gate = pl.BlockSpec((PEER_HEADS, PEER_NKEYS, tl), lambda i: (0, 0, i))
    gshape = jax.ShapeDtypeStruct((PEER_HEADS, PEER_NKEYS, t), F32)
    a, s0, b, s1, thr = pl.pallas_call(
        _peer_route_kernel,
        grid=(t // tl,),
        in_specs=[pl.BlockSpec((d, tl), lambda i: (0, i)),
                  pl.BlockSpec((nq, d), lambda i: (0, 0)),
                  pl.BlockSpec((PEER_HEADS, 2, PEER_NKEYS, PEER_NKEYS), lambda i: (0, 0, 0, 0))],
        out_specs=[gate, gate, gate, gate, pl.BlockSpec((PEER_HEADS, tl), lambda i: (0, i))],
        out_shape=[gshape, gshape, gshape, gshape, jax.ShapeDtypeStruct((PEER_HEADS, t), F32)],
        scratch_shapes=[pltpu.VMEM((nq, tl), F32)],
        compiler_params=_params("parallel"),
        name="peer_route",
    )(h_t, wq_t, keys)

    tl, te = min(512, t), PEER_TE
    gate = pl.BlockSpec((PEER_HEADS, PEER_NKEYS, tl), lambda i, e: (0, 0, i))
    return pl.pallas_call(
        functools.partial(_peer_dense_kernel, tl=tl),
        grid=(t // tl, PEER_EXPERTS // te),
        in_specs=[pl.BlockSpec((d, tl), lambda i, e: (0, i)),
                  pl.BlockSpec((te, d), lambda i, e: (e, 0)),
                  pl.BlockSpec((d, te), lambda i, e: (0, e)),
                  gate, gate, gate, gate,
                  pl.BlockSpec((PEER_HEADS, tl), lambda i, e: (0, i))],
        out_specs=pl.BlockSpec((d, tl), lambda i, e: (0, i)),
        out_shape=jax.ShapeDtypeStruct((d, t), F32),
        scratch_shapes=[pltpu.VMEM((te, tl), F32),
                        pltpu.VMEM((te, tl), BF16)],
        compiler_params=_params("parallel", "arbitrary"),
        name="peer_dense",
    )(h_t, u_bf, vt_bf, a, s0, b, s1, thr)


def kernel(x, c, w_ada, b_ada, w_in, ssm_lam_re, ssm_lam_im, ssm_log_step, ssm_b_re, ssm_b_im, ssm_c_re, ssm_c_im, ssm_d, ssm_w_glu, dn_conv_w, dn_a_log, dn_dt_bias, dn_norm_w, w_out, ln1_g, ln1_b, peer_w_query, peer_sub_keys, peer_u, peer_v, ln2_g, ln2_b):
    bsz, seq, d = x.shape
    depth = w_ada.shape[0]
    t = bsz * seq
    alpha = (2.0 * depth) ** 0.25

    mod = _ada_mod(c, w_ada, b_ada)[:, :bsz]
    xf = x.reshape(t, d)
    for l in range(depth):
        sh1, sc1, gt1, sh2, sc2, gt2 = [m.reshape(bsz, 1, d) for m in jnp.split(mod[l], 6, axis=-1)]

        hmix = _ln_mod(xf, sh1, sc1, seq)
        w_in_bf = w_in[l].astype(BF16)
        proj = _mm(hmix, w_in_bf[:, :N_MAIN], F32, 512, 1024, "in_proj")
        w_gate = jnp.zeros((d, LANES), BF16).at[:, :2 * N_HEADS].set(w_in_bf[:, N_MAIN:])
        gates_raw = _mm(hmix, w_gate, F32, 512, LANES, "gate_proj")

        prep = _s5_prep(ssm_lam_re[l], ssm_lam_im[l], ssm_log_step[l], ssm_b_re[l], ssm_b_im[l],
                        ssm_c_re[l], ssm_c_im[l], ssm_d[l])
        y_ssm = _s5_mixer(proj, bsz, seq, prep, ssm_w_glu[l].astype(BF16))
        y_dn = _dn_mixer(proj, gates_raw, bsz, seq, dn_conv_w[l], dn_a_log[l], dn_dt_bias[l],
                         dn_norm_w[l])
        xf = _out_ln(y_ssm, y_dn, w_out[l].astype(BF16), xf, gt1, ln1_g[l], ln1_b[l], seq, alpha)

        hffn_t = _ln_mod(xf, sh2, sc2, seq).T
        y_t = _peer_ffn(hffn_t, peer_w_query[l].T.astype(BF16), peer_sub_keys[l],
                        peer_u[l].astype(BF16), peer_v[l].T.astype(BF16))
        xf = _res_ln(xf, y_t.T, gt2, ln2_g[l], ln2_b[l], seq, alpha)
    return xf.reshape(bsz, seq, d)
```

```python
import functools
import math

import jax
import jax.numpy as jnp
from jax import lax
from jax.experimental import pallas as pl
from jax.experimental.pallas import tpu as pltpu

F32 = jnp.float32
BF16 = jnp.bfloat16
HI = lax.Precision.HIGHEST

D_MODEL = 2048
D_SSM = 1024
SSM_GROUP = 16
N_GROUPS = D_SSM // SSM_GROUP
SSM_STATE = 64
SSM_CHUNK = 16
N_HEADS = 8
HEAD_DIM = 128
D_DN = N_HEADS * HEAD_DIM
DN_CHUNK = 64
CONV_WIDTH = 4
N_MAIN = D_SSM + 4 * D_DN
PEER_HEADS = 8
PEER_NKEYS = 128
PEER_EXPERTS = PEER_NKEYS * PEER_NKEYS
PEER_TOPK = 16
PEER_SUB = 8
PEER_TE = PEER_SUB * PEER_NKEYS
LN_EPS = 1e-5
NORM_EPS = 1e-6

LANES = 128
VMEM_LIMIT = 56 * 1024 * 1024


def _params(*sem):
    return pltpu.CompilerParams(dimension_semantics=sem, vmem_limit_bytes=VMEM_LIMIT)


def _gelu(x):
    return 0.5 * x * (1.0 + jnp.tanh(math.sqrt(2.0 / math.pi) * (x + 0.044715 * (x * x * x))))


def _sigmoid(x):
    return 1.0 / (1.0 + jnp.exp(-x))


def _dot(a, b, precision=None):
    return jnp.dot(a, b, precision=precision, preferred_element_type=F32)


def _dot_nt(a, b, precision=None):
    return lax.dot_general(a, b, (((1,), (1,)), ((), ())), precision=precision,
                           preferred_element_type=F32)


def _dot_tn(a, b, precision=None):
    return lax.dot_general(a, b, (((0,), (0,)), ((), ())), precision=precision,
                           preferred_element_type=F32)


def _ada_kernel(c_ref, w_ref, b_ref, o_ref):
    c = c_ref[...]
    ca = c * _sigmoid(c)
    hi = ca.astype(BF16)
    lo = (ca - hi.astype(F32)).astype(BF16)
    w = w_ref[0].astype(BF16)
    o_ref[0] = _dot(hi, w) + _dot(lo, w) + b_ref[0]


def _ada_mod(c, w_ada, b_ada):
    depth, d, n = w_ada.shape
    tn = 1024
    cp = jnp.zeros((8, d), F32).at[: c.shape[0]].set(c)
    return pl.pallas_call(
        _ada_kernel,
        grid=(depth, n // tn),
        in_specs=[pl.BlockSpec((8, d), lambda l, j: (0, 0)),
                  pl.BlockSpec((1, d, tn), lambda l, j: (l, 0, j)),
                  pl.BlockSpec((1, 1, tn), lambda l, j: (l, 0, j))],
        out_specs=pl.BlockSpec((1, 8, tn), lambda l, j: (l, 0, j)),
        out_shape=jax.ShapeDtypeStruct((depth, 8, n), F32),
        compiler_params=_params("parallel", "parallel"),
        name="ada_mod",
    )(cp, w_ada, b_ada.reshape(depth, 1, n))


def _ln(x):
    mu = jnp.mean(x, axis=-1, keepdims=True)
    xc = x - mu
    var = jnp.mean(xc * xc, axis=-1, keepdims=True)
    return xc * lax.rsqrt(var + LN_EPS)


def _ln_mod_kernel(x_ref, sh_ref, sc_ref, o_ref):
    o_ref[...] = (_ln(x_ref[...]) * (1.0 + sc_ref[0]) + sh_ref[0]).astype(o_ref.dtype)


def _ln_mod(x, sh, sc, seq):
    t, d = x.shape
    tm = min(256, seq)
    per = seq // tm
    return pl.pallas_call(
        _ln_mod_kernel,
        grid=(t // tm,),
        in_specs=[pl.BlockSpec((tm, d), lambda i: (i, 0)),
                  pl.BlockSpec((1, 1, d), lambda i: (i // per, 0, 0)),
                  pl.BlockSpec((1, 1, d), lambda i: (i // per, 0, 0))],
        out_specs=pl.BlockSpec((tm, d), lambda i: (i, 0)),
        out_shape=jax.ShapeDtypeStruct((t, d), BF16),
        compiler_params=_params("parallel"),
        name="ln_mod",
    )(x, sh, sc)


def _res_ln_kernel(x_ref, y_ref, gt_ref, g_ref, b_ref, o_ref, *, alpha):
    r = alpha * x_ref[...] + gt_ref[0] * y_ref[...]
    o_ref[...] = _ln(r) * g_ref[...] + b_ref[...]


def _res_ln(x, y, gt, gain, bias, seq, alpha):
    t, d = x.shape
    tm = min(256, seq)
    per = seq // tm
    return pl.pallas_call(
        functools.partial(_res_ln_kernel, alpha=alpha),
        grid=(t // tm,),
        in_specs=[pl.BlockSpec((tm, d), lambda i: (i, 0)),
                  pl.BlockSpec((tm, d), lambda i: (i, 0)),
                  pl.BlockSpec((1, 1, d), lambda i: (i // per, 0, 0)),
                  pl.BlockSpec((1, d), lambda i: (0, 0)),
                  pl.BlockSpec((1, d), lambda i: (0, 0))],
        out_specs=pl.BlockSpec((tm, d), lambda i: (i, 0)),
        out_shape=jax.ShapeDtypeStruct((t, d), F32),
        compiler_params=_params("parallel"),
        name="res_ln",
    )(x, y, gt, gain.reshape(1, d), bias.reshape(1, d))


def _mm_kernel(a_ref, b_ref, o_ref):
    o_ref[...] = _dot(a_ref[...], b_ref[...]).astype(o_ref.dtype)


def _mm(a, b, out_dtype, tm, tn, name):
    m, k = a.shape
    n = b.shape[1]
    tm, tn = min(tm, m), min(tn, n)
    return pl.pallas_call(
        _mm_kernel,
        grid=(m // tm, n // tn),
        in_specs=[pl.BlockSpec((tm, k), lambda i, j: (i, 0)),
                  pl.BlockSpec((k, tn), lambda i, j: (0, j))],
        out_specs=pl.BlockSpec((tm, tn), lambda i, j: (i, j)),
        out_shape=jax.ShapeDtypeStruct((m, n), out_dtype),
        compiler_params=_params("parallel", "parallel"),
        name=name,
    )(a, b)


def _s5_prep(lam_re, lam_im, log_step, b_re, b_im, c_re, c_im, d_skip):
    g, p, nch = N_GROUPS, SSM_STATE, SSM_CHUNK
    step = jnp.exp(log_step)[:, None]
    zr, zi = lam_re * step, lam_im * step
    ks = jnp.arange(nch + 1, dtype=F32)[:, None, None]
    mag = jnp.exp(ks * zr)
    pr, pi = mag * jnp.cos(ks * zi), mag * jnp.sin(ks * zi)
    nr, ni = pr[1] - 1.0, pi[1]
    den = lam_re * lam_re + lam_im * lam_im
    fr = (nr * lam_re + ni * lam_im) / den
    fi = (ni * lam_re - nr * lam_im) / den
    bbr = fr[..., None] * b_re - fi[..., None] * b_im
    bbi = fr[..., None] * b_im + fi[..., None] * b_re
    er = pr[:nch, :, :, None] * bbr - pi[:nch, :, :, None] * bbi
    ei = pr[:nch, :, :, None] * bbi + pi[:nch, :, :, None] * bbr
    kk = (jnp.einsum('gap,kgpc->kgac', c_re, er, precision=HI)
          - jnp.einsum('gap,kgpc->kgac', c_im, ei, precision=HI))
    dlt = jnp.arange(nch)[None, :] - jnp.arange(nch)[:, None]
    m = jnp.where((dlt >= 0)[:, :, None, None, None], kk[jnp.clip(dlt, 0, nch - 1)], 0.0)
    m_intra = m.transpose(2, 0, 4, 1, 3).reshape(g, nch * SSM_GROUP, nch * SSM_GROUP)
    m_state = jnp.concatenate([er[::-1].transpose(1, 0, 3, 2), ei[::-1].transpose(1, 0, 3, 2)],
                              axis=-1).reshape(g, nch * SSM_GROUP, 2 * p)
    wr = c_re[None] * pr[1:, :, None, :] - c_im[None] * pi[1:, :, None, :]
    wi = c_re[None] * pi[1:, :, None, :] + c_im[None] * pr[1:, :, None, :]
    m_out = jnp.concatenate([wr.transpose(1, 3, 0, 2), -wi.transpose(1, 3, 0, 2)],
                            axis=1).reshape(g, 2 * p, nch * SSM_GROUP)
    a_re, a_im = pr[nch].reshape(1, g * p), pi[nch].reshape(1, g * p)
    dvec = jnp.tile(d_skip.reshape(g, 1, SSM_GROUP), (1, nch, 1)).reshape(g, 1, nch * SSM_GROUP)
    return m_state, m_intra, m_out, a_re, a_im, dvec


def _s5_state_kernel(x_ref, m_ref, o_ref):
    o_ref[0] = _dot(x_ref[0], m_ref[0], HI)


def _s5_scan_kernel(s_ref, ar_ref, ai_ref, h_ref):
    nchunks = s_ref.shape[0]
    ar, ai = ar_ref[...], ai_ref[...]

    def body(c, carry):
        hr, hi = carry
        h_ref[c, 0] = hr
        h_ref[c, 1] = hi
        sr, si = s_ref[c, 0], s_ref[c, 1]
        return ar * hr - ai * hi + sr, ar * hi + ai * hr + si

    zero = jnp.zeros(s_ref.shape[2:], F32)
    lax.fori_loop(0, nchunks, body, (zero, zero))


def _s5_out_kernel(x_ref, h_ref, mi_ref, mo_ref, d_ref, o_ref):
    x = x_ref[0]
    y = _dot(x, mi_ref[0], HI) + _dot(h_ref[0], mo_ref[0], HI) + d_ref[0] * x
    o_ref[0] = _gelu(y).astype(o_ref.dtype)


def _glu_kernel(y_ref, wa_ref, wb_ref, o_ref):
    y = y_ref[...]
    o_ref[...] = (_dot(y, wa_ref[...]) * _sigmoid(_dot(y, wb_ref[...]))).astype(o_ref.dtype)


def _s5_mixer(proj, bsz, seq, prep, w_glu):
    g, p, nch = N_GROUPS, SSM_STATE, SSM_CHUNK
    m_state, m_intra, m_out, a_re, a_im, dvec = prep
    ncs = seq // nch
    rows = ncs * bsz
    width = nch * SSM_GROUP
    xg = proj[:, :D_SSM].reshape(bsz, ncs, nch, g, SSM_GROUP).transpose(3, 1, 0, 2, 4)
    xg = xg.reshape(g, rows, width)

    s_loc = pl.pallas_call(
        _s5_state_kernel,
        grid=(g,),
        in_specs=[pl.BlockSpec((1, rows, width), lambda i: (i, 0, 0)),
                  pl.BlockSpec((1, width, 2 * p), lambda i: (i, 0, 0))],
        out_specs=pl.BlockSpec((1, rows, 2 * p), lambda i: (i, 0, 0)),
        out_shape=jax.ShapeDtypeStruct((g, rows, 2 * p), F32),
        compiler_params=_params("parallel"),
        name="s5_state",
    )(xg, m_state)

    s_t = s_loc.reshape(g, ncs, bsz, 2, p).transpose(1, 3, 2, 0, 4).reshape(ncs, 2, bsz, g * p)
    lb = 1024
    h_t = pl.pallas_call(
        _s5_scan_kernel,
        grid=(g * p // lb,),
        in_specs=[pl.BlockSpec((ncs, 2, bsz, lb), lambda i: (0, 0, 0, i)),
                  pl.BlockSpec((1, lb), lambda i: (0, i)),
                  pl.BlockSpec((1, lb), lambda i: (0, i))],
        out_specs=pl.BlockSpec((ncs, 2, bsz, lb), lambda i: (0, 0, 0, i)),
        out_shape=jax.ShapeDtypeStruct((ncs, 2, bsz, g * p), F32),
        compiler_params=_params("parallel"),
        name="s5_scan",
    )(s_t, a_re, a_im)
    h_prev = h_t.reshape(ncs, 2, bsz, g, p).transpose(3, 0, 2, 1, 4).reshape(g, rows, 2 * p)

    yg = pl.pallas_call(
        _s5_out_kernel,
        grid=(g,),
        in_specs=[pl.BlockSpec((1, rows, width), lambda i: (i, 0, 0)),
                  pl.BlockSpec((1, rows, 2 * p), lambda i: (i, 0, 0)),
                  pl.BlockSpec((1, width, width), lambda i: (i, 0, 0)),
                  pl.BlockSpec((1, 2 * p, width), lambda i: (i, 0, 0)),
                  pl.BlockSpec((1, 1, width), lambda i: (i, 0, 0))],
        out_specs=pl.BlockSpec((1, rows, width), lambda i: (i, 0, 0)),
        out_shape=jax.ShapeDtypeStruct((g, rows, width), BF16),
        compiler_params=_params("parallel"),
        name="s5_out",
    )(xg, h_prev, m_intra, m_out, dvec)
    y = yg.reshape(g, ncs, bsz, nch, SSM_GROUP).transpose(2, 1, 3, 0, 4).reshape(bsz * seq, D_SSM)

    t = bsz * seq
    tm, tn = min(512, t), 512
    nj = D_SSM // tn
    return pl.pallas_call(
        _glu_kernel,
        grid=(t // tm, nj),
        in_specs=[pl.BlockSpec((tm, D_SSM), lambda i, j: (i, 0)),
                  pl.BlockSpec((D_SSM, tn), lambda i, j: (0, j)),
                  pl.BlockSpec((D_SSM, tn), lambda i, j: (0, j + nj))],
        out_specs=pl.BlockSpec((tm, tn), lambda i, j: (i, j)),
        out_shape=jax.ShapeDtypeStruct((t, D_SSM), BF16),
        compiler_params=_params("parallel", "parallel"),
        name="s5_glu",
    )(y, w_glu, w_glu)


def _dn_conv_kernel(x_ref, w_ref, o_ref):
    j = pl.program_id(1)
    x = x_ref[0]
    w = w_ref[...]
    row = lax.broadcasted_iota(jnp.int32, x.shape, 0)
    acc = x * w[CONV_WIDTH - 1:CONV_WIDTH, :]
    for k in range(CONV_WIDTH - 1):
        sh = CONV_WIDTH - 1 - k
        xs = jnp.where(row >= sh, pltpu.roll(x, sh, axis=0), 0.0)
        acc = acc + xs * w[k:k + 1, :]
    y = acc * _sigmoid(acc)
    nrm = y * lax.rsqrt(jnp.sum(y * y, axis=-1, keepdims=True) + NORM_EPS)
    qscale = jnp.where(j < N_HEADS, HEAD_DIM ** -0.5, 1.0).astype(F32)
    o_ref[0] = jnp.where(j < 2 * N_HEADS, nrm * qscale, y)


def _gates_kernel(x_ref, alog_ref, dtb_ref, o_ref):
    x = x_ref[...]
    lane = lax.broadcasted_iota(jnp.int32, x.shape, 1)
    xs = x + dtb_ref[...]
    softplus = jnp.maximum(xs, 0.0) + jnp.log1p(jnp.exp(-jnp.abs(xs)))
    o_ref[...] = jnp.where(lane < N_HEADS, _sigmoid(x), -jnp.exp(alog_ref[...]) * softplus)


def _dn_kernel(q_ref, k_ref, v_ref, z_ref, gt_ref, nw_ref, o_ref, st_s, *, nchunks):
    c, dh = DN_CHUNK, HEAD_DIM

    @pl.when(pl.program_id(1) == 0)
    def _():
        st_s[...] = jnp.zeros_like(st_s)

    ri = lax.broadcasted_iota(jnp.int32, (c, c), 0)
    ci = lax.broadcasted_iota(jnp.int32, (c, c), 1)
    tril, strict = ri >= ci, ri > ci
    tril_f = tril.astype(F32)
    triu_f = (ri <= ci).astype(F32)
    eye = (ri == ci).astype(F32)
    bf = lambda x: x.astype(BF16)

    def chunk(n, carry):
        rows = pl.ds(pl.multiple_of(n * c, c), c)
        gt = gt_ref[0, rows, :]
        gcum_c = _dot(tril_f, gt, HI)
        gcum_r = _dot_tn(gt, triu_f, HI)
        heads = range(N_HEADS)
        cols = [slice(h * dh, (h + 1) * dh) for h in heads]
        beta = [gt[:, h:h + 1] for h in heads]
        gc = [gcum_c[:, N_HEADS + h:N_HEADS + h + 1] for h in heads]
        gr = [gcum_r[N_HEADS + h:N_HEADS + h + 1, :] for h in heads]
        gc_b = [jnp.broadcast_to(gc[h], (c, dh)) for h in heads]
        g_last = [gc_b[h][c - 1:c, :] for h in heads]
        decay = [jnp.where(tril, jnp.exp(jnp.where(tril, gc[h] - gr[h], 0.0)), 0.0) for h in heads]
        q = [q_ref[0, rows, cols[h]] for h in heads]
        k = [k_ref[0, rows, cols[h]] for h in heads]
        kb = [k[h] * beta[h] for h in heads]
        k16 = [bf(k[h]) for h in heads]
        lmat = [jnp.where(strict, _dot_nt(bf(kb[h]), k16[h]) * decay[h], 0.0) for h in heads]
        attn = [bf(jnp.where(tril, _dot_nt(bf(q[h]), k16[h]) * decay[h], 0.0)) for h in heads]
        tmat = [eye - lmat[h] for h in heads]
        lp16 = [bf(lmat[h]) for h in heads]
        lp16 = [bf(_dot(lp16[h], lp16[h])) for h in heads]
        for step in range(5):
            tmat = [tmat[h] + _dot(bf(tmat[h]), lp16[h]) for h in heads]
            if step < 4:
                lp16 = [bf(_dot(lp16[h], lp16[h])) for h in heads]
        t16 = [bf(tmat[h]) for h in heads]
        egc = [jnp.exp(gc_b[h]) for h in heads]
        w_val = [_dot(t16[h], bf(v_ref[0, rows, cols[h]] * beta[h])) for h in heads]
        k_cum = [bf(_dot(t16[h], bf(kb[h] * egc[h]))) for h in heads]
        state = [st_s[h] for h in heads]
        s16 = [bf(state[h]) for h in heads]
        out = [_dot(bf(q[h] * egc[h]), s16[h]) for h in heads]
        vn16 = [bf(w_val[h] - _dot(k_cum[h], s16[h])) for h in heads]
        out = [out[h] + _dot(attn[h], vn16[h]) for h in heads]
        for h in heads:
            k_tail = bf(k[h] * jnp.exp(g_last[h] - gc_b[h]))
            st_s[h] = state[h] * jnp.exp(g_last[h]) + _dot_tn(k_tail, vn16[h])
        for h in heads:
            o = out[h] * lax.rsqrt(jnp.mean(out[h] * out[h], axis=-1, keepdims=True) + NORM_EPS)
            z = z_ref[0, rows, cols[h]]
            o_ref[0, rows, cols[h]] = (o * nw_ref[...] * (z * _sigmoid(z))).astype(o_ref.dtype)
        return carry

    lax.fori_loop(0, nchunks, chunk, 0)


def _dn_mixer(proj, gates_raw, bsz, seq, conv_w, a_log, dt_bias, norm_w):
    t = bsz * seq
    proj3 = proj.reshape(bsz, seq, N_MAIN)
    qkv_off = D_SSM // LANES
    n_tiles = 3 * D_DN // LANES
    qkv = pl.pallas_call(
        _dn_conv_kernel,
        grid=(bsz, n_tiles),
        in_specs=[pl.BlockSpec((1, seq, LANES), lambda b, j: (b, 0, j + qkv_off)),
                  pl.BlockSpec((CONV_WIDTH, LANES), lambda b, j: (0, j))],
        out_specs=pl.BlockSpec((1, seq, LANES), lambda b, j: (b, 0, j)),
        out_shape=jax.ShapeDtypeStruct((bsz, seq, 3 * D_DN), F32),
        compiler_params=_params("parallel", "parallel"),
        name="dn_conv",
    )(proj3, conv_w)

    pad = jnp.zeros((1, LANES), F32)
    alog_p = pad.at[0, N_HEADS:2 * N_HEADS].set(a_log)
    dtb_p = pad.at[0, N_HEADS:2 * N_HEADS].set(dt_bias)
    tm = min(1024, t)
    gates = pl.pallas_call(
        _gates_kernel,
        grid=(t // tm,),
        in_specs=[pl.BlockSpec((tm, LANES), lambda i: (i, 0)),
                  pl.BlockSpec((1, LANES), lambda i: (0, 0)),
                  pl.BlockSpec((1, LANES), lambda i: (0, 0))],
        out_specs=pl.BlockSpec((tm, LANES), lambda i: (i, 0)),
        out_shape=jax.ShapeDtypeStruct((t, LANES), F32),
        compiler_params=_params("parallel"),
        name="dn_gates",
    )(gates_raw, alog_p, dtb_p).reshape(bsz, seq, LANES)

    tb = min(512, seq)
    part = lambda off: pl.BlockSpec((1, tb, D_DN), lambda b, i: (b, i, off))
    return pl.pallas_call(
        functools.partial(_dn_kernel, nchunks=tb // DN_CHUNK),
        grid=(bsz, seq // tb),
        in_specs=[part(0), part(1), part(2), part((D_SSM + 3 * D_DN) // D_DN),
                  pl.BlockSpec((1, tb, LANES), lambda b, i: (b, i, 0)),
                  pl.BlockSpec((1, HEAD_DIM), lambda b, i: (0, 0))],
        out_specs=pl.BlockSpec((1, tb, D_DN), lambda b, i: (b, i, 0)),
        out_shape=jax.ShapeDtypeStruct((bsz, seq, D_DN), BF16),
        scratch_shapes=[pltpu.VMEM((N_HEADS, HEAD_DIM, HEAD_DIM), F32)],
        compiler_params=_params("parallel", "arbitrary"),
        name="dn_chunk",
    )(qkv, qkv, qkv, proj3, gates, norm_w.reshape(1, HEAD_DIM)).reshape(t, D_DN)


def _out_ln_kernel(a1_ref, a2_ref, w1_ref, w2_ref, x_ref, gt_ref, g_ref, b_ref, o_ref, *, alpha):
    y = _dot(a1_ref[...], w1_ref[...]) + _dot(a2_ref[...], w2_ref[...])
    r = alpha * x_ref[...] + gt_ref[0] * y
    o_ref[...] = _ln(r) * g_ref[...] + b_ref[...]


def _out_ln(y_ssm, y_dn, w_out, x, gt, gain, bias, seq, alpha):
    t, d = x.shape
    tm = min(256, seq)
    per = seq // tm
    return pl.pallas_call(
        functools.partial(_out_ln_kernel, alpha=alpha),
        grid=(t // tm,),
        in_specs=[pl.BlockSpec((tm, D_SSM), lambda i: (i, 0)),
                  pl.BlockSpec((tm, D_DN), lambda i: (i, 0)),
                  pl.BlockSpec((D_SSM, d), lambda i: (0, 0)),
                  pl.BlockSpec((D_DN, d), lambda i: (1, 0)),
                  pl.BlockSpec((tm, d), lambda i: (i, 0)),
                  pl.BlockSpec((1, 1, d), lambda i: (i // per, 0, 0)),
                  pl.BlockSpec((1, d), lambda i: (0, 0)),
                  pl.BlockSpec((1, d), lambda i: (0, 0))],
        out_specs=pl.BlockSpec((tm, d), lambda i: (i, 0)),
        out_shape=jax.ShapeDtypeStruct((t, d), F32),
        compiler_params=_params("parallel"),
        name="out_ln",
    )(y_ssm, y_dn, w_out, w_out, x, gt, gain.reshape(1, d), bias.reshape(1, d))


def _top_values(s, count):
    vals = []
    for r in range(count):
        m = jnp.max(s, axis=0, keepdims=True)
        vals.append(m)
        if r + 1 < count:
            s = jnp.where(s == m, -jnp.inf, s)
    return vals


def _peer_route_kernel(ht_ref, wq_ref, keys_ref, a_ref, cnt_ref, b_ref, rank_ref, q_s):
    q_s[...] = _dot(wq_ref[...], ht_ref[...])
    k = PEER_TOPK

    def head(h, carry):
        base = pl.multiple_of(h * 2 * PEER_NKEYS, 2 * PEER_NKEYS)
        s0 = _dot(keys_ref[h, 0], q_s[pl.ds(base, PEER_NKEYS), :], HI)
        s1 = _dot(keys_ref[h, 1], q_s[pl.ds(base + PEER_NKEYS, PEER_NKEYS), :], HI)
        v0, v1 = _top_values(s0, k), _top_values(s1, k)
        v1m = jnp.concatenate(v1, axis=0)
        cands = [v0[i] + v1m for i in range(k)]
        top = v0[0] + v1[0]
        z = jnp.zeros_like(top)
        thr = top
        for r in range(k):
            m = cands[0]
            for cnd in cands[1:]:
                m = jnp.maximum(m, cnd)
            m = jnp.max(m, axis=0, keepdims=True)
            z = z + jnp.exp(m - top)
            thr = m
            if r + 1 < k:
                cands = [jnp.where(cnd == m, -jnp.inf, cnd) for cnd in cands]
        cnt = jnp.zeros_like(s0)
        rank = jnp.zeros_like(s1)
        for r in range(k):
            cnt = cnt + jnp.where(s0 + v1[r] >= thr, 1.0, 0.0)
            rank = rank + jnp.where(v1[r] > s1, 1.0, 0.0)
        a_ref[h] = jnp.exp(s0 - v0[0]) / z
        b_ref[h] = jnp.exp(s1 - v1[0])
        cnt_ref[h] = cnt
        rank_ref[h] = rank
        return carry

    lax.fori_loop(0, PEER_HEADS, head, 0)


def _peer_dense_kernel(ht_ref, u_ref, vt_ref, a_ref, cnt_ref, b_ref, rank_ref, o_ref,
                       act_s, p_s, *, tl):
    e = pl.program_id(1)

    @pl.when(e == 0)
    def _():
        o_ref[...] = jnp.zeros_like(o_ref)

    act_s[...] = _gelu(_dot(u_ref[...], ht_ref[...]))
    first = pl.ds(pl.multiple_of(e * PEER_SUB, PEER_SUB), PEER_SUB)
    for lc in range(tl // LANES):
        ls = slice(lc * LANES, (lc + 1) * LANES)
        cnt_rows = [cnt_ref[h, first, ls] for h in range(PEER_HEADS)]
        a_rows = [a_ref[h, first, ls] for h in range(PEER_HEADS)]
        for ii in range(PEER_SUB):
            er = slice(ii * PEER_NKEYS, (ii + 1) * PEER_NKEYS)
            w = jnp.zeros((PEER_NKEYS, LANES), F32)
            for h in range(PEER_HEADS):
                keep = rank_ref[h, :, ls] < cnt_rows[h][ii:ii + 1, :]
                w = w + jnp.where(keep, b_ref[h, :, ls], 0.0) * a_rows[h][ii:ii + 1, :]
            p_s[er, ls] = (act_s[er, ls] * w).astype(BF16)
    o_ref[...] += _dot(vt_ref[...], p_s[...])


def _peer_ffn(h_t, wq_t, keys, u_bf, vt_bf):
    d, t = h_t.shape
    nq = wq_t.shape[0]
    tl = min(256, t)
    gate = pl.BlockSpec((PEER_HEADS, PEER_NKEYS, tl), lambda i: (0, 0, i))
    gshape = jax.ShapeDtypeStruct((PEER_HEADS, PEER_NKEYS, t), F32)
    a, cnt, b, rank = pl.pallas_call(
        _peer_route_kernel,
        grid=(t // tl,),
        in_specs=[pl.BlockSpec((d, tl), lambda i: (0, i)),
                  pl.BlockSpec((nq, d), lambda i: (0, 0)),
                  pl.BlockSpec((PEER_HEADS, 2, PEER_NKEYS, PEER_NKEYS), lambda i: (0, 0, 0, 0))],
        out_specs=[gate, gate, gate, gate],
        out_shape=[gshape, gshape, gshape, gshape],
        scratch_shapes=[pltpu.VMEM((nq, tl), F32)],
        compiler_params=_params("parallel"),
        name="peer_route",
    )(h_t, wq_t, keys)

    tl, te = min(512, t), PEER_TE
    gate = pl.BlockSpec((PEER_HEADS, PEER_NKEYS, tl), lambda i, e: (0, 0, i))
    return pl.pallas_call(
        functools.partial(_peer_dense_kernel, tl=tl),
        grid=(t // tl, PEER_EXPERTS // te),
        in_specs=[pl.BlockSpec((d, tl), lambda i, e: (0, i)),
                  pl.BlockSpec((te, d), lambda i, e: (e, 0)),
                  pl.BlockSpec((d, te), lambda i, e: (0, e)),
                  gate, gate, gate, gate],
        out_specs=pl.BlockSpec((d, tl), lambda i, e: (0, i)),
        out_shape=jax.ShapeDtypeStruct((d, t), F32),
        scratch_shapes=[pltpu.VMEM((te, tl), F32),
                        pltpu.VMEM((te, tl), BF16)],
        compiler_params=_params("parallel", "arbitrary"),
        name="peer_dense",
    )(h_t, u_bf, vt_bf, a, cnt, b, rank)


def kernel(x, c, w_ada, b_ada, w_in, ssm_lam_re, ssm_lam_im, ssm_log_step, ssm_b_re, ssm_b_im, ssm_c_re, ssm_c_im, ssm_d, ssm_w_glu, dn_conv_w, dn_a_log, dn_dt_bias, dn_norm_w, w_out, ln1_g, ln1_b, peer_w_query, peer_sub_keys, peer_u, peer_v, ln2_g, ln2_b):
    bsz, seq, d = x.shape
    depth = w_ada.shape[0]
    t = bsz * seq
    alpha = (2.0 * depth) ** 0.25

    mod = _ada_mod(c, w_ada, b_ada)[:, :bsz]
    xf = x.reshape(t, d)
    for l in range(depth):
        sh1, sc1, gt1, sh2, sc2, gt2 = [m.reshape(bsz, 1, d) for m in jnp.split(mod[l], 6, axis=-1)]

        hmix = _ln_mod(xf, sh1, sc1, seq)
        w_in_bf = w_in[l].astype(BF16)
        proj = _mm(hmix, w_in_bf[:, :N_MAIN], F32, 512, 1024, "in_proj")
        w_gate = jnp.zeros((d, LANES), BF16).at[:, :2 * N_HEADS].set(w_in_bf[:, N_MAIN:])
        gates_raw = _mm(hmix, w_gate, F32, 512, LANES, "gate_proj")

        prep = _s5_prep(ssm_lam_re[l], ssm_lam_im[l], ssm_log_step[l], ssm_b_re[l], ssm_b_im[l],
                        ssm_c_re[l], ssm_c_im[l], ssm_d[l])
        y_ssm = _s5_mixer(proj, bsz, seq, prep, ssm_w_glu[l].astype(BF16))
        y_dn = _dn_mixer(proj, gates_raw, bsz, seq, dn_conv_w[l], dn_a_log[l], dn_dt_bias[l],
                         dn_norm_w[l])
        xf = _out_ln(y_ssm, y_dn, w_out[l].astype(BF16), xf, gt1, ln1_g[l], ln1_b[l], seq, alpha)

        hffn_t = _ln_mod(xf, sh2, sc2, seq).T
        y_t = _peer_ffn(hffn_t, peer_w_query[l].T.astype(BF16), peer_sub_keys[l],
                        peer_u[l].astype(BF16), peer_v[l].T.astype(BF16))
        xf = _res_ln(xf, y_t.T, gt2, ln2_g[l], ln2_b[l], seq, alpha)
    return xf.reshape(bsz, seq, d)
```

```python
import functools
import math

import jax
import jax.numpy as jnp
from jax import lax
from jax.experimental import pallas as pl
from jax.experimental.pallas import tpu as pltpu

F32 = jnp.float32
BF16 = jnp.bfloat16
HI = lax.Precision.HIGHEST

D_MODEL = 2048
D_SSM = 1024
SSM_GROUP = 16
N_GROUPS = D_SSM // SSM_GROUP
SSM_STATE = 64
SSM_CHUNK = 16
N_HEADS = 8
HEAD_DIM = 128
D_DN = N_HEADS * HEAD_DIM
DN_CHUNK = 64
CONV_WIDTH = 4
N_MAIN = D_SSM + 4 * D_DN
PEER_HEADS = 8
PEER_NKEYS = 128
PEER_EXPERTS = PEER_NKEYS * PEER_NKEYS
PEER_TOPK = 16
PEER_SUB = 8
PEER_TE = PEER_SUB * PEER_NKEYS
LN_EPS = 1e-5
NORM_EPS = 1e-6

LANES = 128
VMEM_LIMIT = 56 * 1024 * 1024


def _params(*sem):
    return pltpu.CompilerParams(dimension_semantics=sem, vmem_limit_bytes=VMEM_LIMIT)


def _gelu(x):
    return 0.5 * x * (1.0 + jnp.tanh(math.sqrt(2.0 / math.pi) * (x + 0.044715 * (x * x * x))))


def _sigmoid(x):
    return 1.0 / (1.0 + jnp.exp(-x))


def _dot(a, b, precision=None):
    return jnp.dot(a, b, precision=precision, preferred_element_type=F32)


def _dot_nt(a, b, precision=None):
    return lax.dot_general(a, b, (((1,), (1,)), ((), ())), precision=precision,
                           preferred_element_type=F32)


def _dot_tn(a, b, precision=None):
    return lax.dot_general(a, b, (((0,), (0,)), ((), ())), precision=precision,
                           preferred_element_type=F32)


def _ada_kernel(c_ref, w_ref, b_ref, o_ref):
    c = c_ref[...]
    ca = c * _sigmoid(c)
    hi = ca.astype(BF16)
    lo = (ca - hi.astype(F32)).astype(BF16)
    w = w_ref[0].astype(BF16)
    o_ref[0] = _dot(hi, w) + _dot(lo, w) + b_ref[0]


def _ada_mod(c, w_ada, b_ada):
    depth, d, n = w_ada.shape
    tn = 1024
    cp = jnp.zeros((8, d), F32).at[: c.shape[0]].set(c)
    return pl.pallas_call(
        _ada_kernel,
        grid=(depth, n // tn),
        in_specs=[pl.BlockSpec((8, d), lambda l, j: (0, 0)),
                  pl.BlockSpec((1, d, tn), lambda l, j: (l, 0, j)),
                  pl.BlockSpec((1, 1, tn), lambda l, j: (l, 0, j))],
        out_specs=pl.BlockSpec((1, 8, tn), lambda l, j: (l, 0, j)),
        out_shape=jax.ShapeDtypeStruct((depth, 8, n), F32),
        compiler_params=_params("parallel", "parallel"),
        name="ada_mod",
    )(cp, w_ada, b_ada.reshape(depth, 1, n))


def _ln(x):
    mu = jnp.mean(x, axis=-1, keepdims=True)
    xc = x - mu
    var = jnp.mean(xc * xc, axis=-1, keepdims=True)
    return xc * lax.rsqrt(var + LN_EPS)


def _ln_mod_kernel(x_ref, sh_ref, sc_ref, o_ref, *, feature_major):
    y = _ln(x_ref[...]) * (1.0 + sc_ref[0]) + sh_ref[0]
    o_ref[...] = (y.T if feature_major else y).astype(o_ref.dtype)


def _ln_mod(x, sh, sc, seq, feature_major=False):
    t, d = x.shape
    tm = min(256, seq)
    per = seq // tm
    if feature_major:
        out_spec, out_shape = pl.BlockSpec((d, tm), lambda i: (0, i)), (d, t)
    else:
        out_spec, out_shape = pl.BlockSpec((tm, d), lambda i: (i, 0)), (t, d)
    return pl.pallas_call(
        functools.partial(_ln_mod_kernel, feature_major=feature_major),
        grid=(t // tm,),
        in_specs=[pl.BlockSpec((tm, d), lambda i: (i, 0)),
                  pl.BlockSpec((1, 1, d), lambda i: (i // per, 0, 0)),
                  pl.BlockSpec((1, 1, d), lambda i: (i // per, 0, 0))],
        out_specs=out_spec,
        out_shape=jax.ShapeDtypeStruct(out_shape, BF16),
        compiler_params=_params("parallel"),
        name="ln_mod",
    )(x, sh, sc)


def _res_ln_kernel(x_ref, yt_ref, gt_ref, g_ref, b_ref, o_ref, *, alpha):
    r = alpha * x_ref[...] + gt_ref[0] * yt_ref[...].T
    o_ref[...] = _ln(r) * g_ref[...] + b_ref[...]


def _res_ln(x, y_t, gt, gain, bias, seq, alpha):
    t, d = x.shape
    tm = min(256, seq)
    per = seq // tm
    return pl.pallas_call(
        functools.partial(_res_ln_kernel, alpha=alpha),
        grid=(t // tm,),
        in_specs=[pl.BlockSpec((tm, d), lambda i: (i, 0)),
                  pl.BlockSpec((d, tm), lambda i: (0, i)),
                  pl.BlockSpec((1, 1, d), lambda i: (i // per, 0, 0)),
                  pl.BlockSpec((1, d), lambda i: (0, 0)),
                  pl.BlockSpec((1, d), lambda i: (0, 0))],
        out_specs=pl.BlockSpec((tm, d), lambda i: (i, 0)),
        out_shape=jax.ShapeDtypeStruct((t, d), F32),
        compiler_params=_params("parallel"),
        name="res_ln",
    )(x, y_t, gt, gain.reshape(1, d), bias.reshape(1, d))


def _mm_kernel(a_ref, b_ref, o_ref):
    o_ref[...] = _dot(a_ref[...], b_ref[...]).astype(o_ref.dtype)


def _mm(a, b, out_dtype, tm, tn, name, n=None):
    m, k = a.shape
    n = b.shape[1] if n is None else n
    tm, tn = min(tm, m), min(tn, n)
    return pl.pallas_call(
        _mm_kernel,
        grid=(m // tm, n // tn),
        in_specs=[pl.BlockSpec((tm, k), lambda i, j: (i, 0)),
                  pl.BlockSpec((k, tn), lambda i, j: (0, j))],
        out_specs=pl.BlockSpec((tm, tn), lambda i, j: (i, j)),
        out_shape=jax.ShapeDtypeStruct((m, n), out_dtype),
        compiler_params=_params("parallel", "parallel"),
        name=name,
    )(a, b)


def _s5_prep(lam_re, lam_im, log_step, b_re, b_im, c_re, c_im, d_skip):
    g, p, nch = N_GROUPS, SSM_STATE, SSM_CHUNK
    gpt = LANES // SSM_GROUP
    ntile = g // gpt
    step = jnp.exp(log_step)[:, None]
    zr, zi = lam_re * step, lam_im * step
    ks = jnp.arange(nch + 1, dtype=F32)[:, None, None]
    mag = jnp.exp(ks * zr)
    pr, pi = mag * jnp.cos(ks * zi), mag * jnp.sin(ks * zi)
    nr, ni = pr[1] - 1.0, pi[1]
    den = lam_re * lam_re + lam_im * lam_im
    fr = (nr * lam_re + ni * lam_im) / den
    fi = (ni * lam_re - nr * lam_im) / den
    bbr = fr[..., None] * b_re - fi[..., None] * b_im
    bbi = fr[..., None] * b_im + fi[..., None] * b_re
    er = pr[:nch, :, :, None] * bbr - pi[:nch, :, :, None] * bbi
    ei = pr[:nch, :, :, None] * bbi + pi[:nch, :, :, None] * bbr
    kk = (jnp.einsum('gap,kgpc->kgac', c_re, er, precision=HI)
          - jnp.einsum('gap,kgpc->kgac', c_im, ei, precision=HI))
    wr = c_re[None] * pr[1:, :, None, :] - c_im[None] * pi[1:, :, None, :]
    wi = c_re[None] * pi[1:, :, None, :] + c_im[None] * pr[1:, :, None, :]
    eye = jnp.eye(gpt, dtype=F32)

    def blockdiag(m, spec, rows, cols):
        return jnp.einsum(spec, m, eye).reshape(nch, ntile, rows, cols).swapaxes(0, 1).astype(BF16)

    split = lambda m: m.reshape(nch, ntile, gpt, *m.shape[2:])
    bs_re = blockdiag(split(er[::-1]), 'stgpc,gh->stgchp', LANES, gpt * p)
    bs_im = blockdiag(split(ei[::-1]), 'stgpc,gh->stgchp', LANES, gpt * p)
    bk = blockdiag(split(kk), 'ktgac,gh->ktgcha', LANES, LANES).reshape(ntile, nch // 2, 2 * LANES, LANES)
    mo_re = blockdiag(split(wr), 'rtgap,gh->rtgpha', gpt * p, LANES)
    mo_im = blockdiag(split(-wi), 'rtgap,gh->rtgpha', gpt * p, LANES)
    a_re, a_im = pr[nch].reshape(1, g * p), pi[nch].reshape(1, g * p)
    return bs_re, bs_im, bk, mo_re, mo_im, a_re, a_im, d_skip.reshape(1, D_SSM)


def _s5_state_kernel(u_ref, bsr_ref, bsi_ref, sr_ref, si_ref):
    ncs = sr_ref.shape[0]
    acc_r = jnp.zeros(sr_ref.shape, F32)
    acc_i = jnp.zeros(si_ref.shape, F32)
    for s in range(SSM_CHUNK):
        xs = u_ref[0, pl.ds(s, ncs, stride=SSM_CHUNK), :].astype(BF16)
        acc_r = acc_r + _dot(xs, bsr_ref[0, s])
        acc_i = acc_i + _dot(xs, bsi_ref[0, s])
    sr_ref[...] = acc_r
    si_ref[...] = acc_i


def _s5_scan_kernel(sr_ref, si_ref, ar_ref, ai_ref, hr_ref, hi_ref):
    ar, ai = ar_ref[...], ai_ref[...]
    sub = 8

    def group(i, carry):
        hr, hi = carry
        rows = pl.ds(pl.multiple_of(i * sub, sub), sub)
        sr, si = sr_ref[rows, :], si_ref[rows, :]
        out_r, out_i = [], []
        for r in range(sub):
            out_r.append(hr)
            out_i.append(hi)
            hr, hi = ar * hr - ai * hi + sr[r:r + 1, :], ar * hi + ai * hr + si[r:r + 1, :]
        hr_ref[rows, :] = jnp.concatenate(out_r, axis=0)
        hi_ref[rows, :] = jnp.concatenate(out_i, axis=0)
        return hr, hi

    zero = jnp.zeros(ar.shape, F32)
    lax.fori_loop(0, sr_ref.shape[0] // sub, group, (zero, zero))


def _s5_out_kernel(u_ref, hr_ref, hi_ref, bk_ref, mor_ref, moi_ref, d_ref, o_ref, acc_s):
    x = u_ref[0]
    ncs = hr_ref.shape[0]
    pos = lax.broadcasted_iota(jnp.int32, x.shape, 0) % SSM_CHUNK

    def lagged(k):
        if k == 0:
            return x.astype(BF16)
        return jnp.where(pos >= k, pltpu.roll(x, k, axis=0), 0.0).astype(BF16)

    acc = d_ref[...] * x
    for kp in range(SSM_CHUNK // 2):
        xx = jnp.concatenate([lagged(2 * kp), lagged(2 * kp + 1)], axis=1)
        acc = acc + _dot(xx, bk_ref[0, kp])
    acc_s[...] = acc
    hr, hi = hr_ref[...].astype(BF16), hi_ref[...].astype(BF16)
    for r in range(SSM_CHUNK):
        rows = pl.ds(r, ncs, stride=SSM_CHUNK)
        acc_s[rows, :] = acc_s[rows, :] + _dot(hr, mor_ref[0, r]) + _dot(hi, moi_ref[0, r])
    o_ref[0] = _gelu(acc_s[...]).astype(o_ref.dtype)


def _glu_kernel(y_ref, wa_ref, wb_ref, o_ref):
    y = y_ref[...]
    o_ref[...] = (_dot(y, wa_ref[...]) * _sigmoid(_dot(y, wb_ref[...]))).astype(o_ref.dtype)


def _s5_mixer(proj, bsz, seq, prep, w_glu):
    bs_re, bs_im, bk, mo_re, mo_im, a_re, a_im, dvec = prep
    t = bsz * seq
    ncs = seq // SSM_CHUNK
    ntile = D_SSM // LANES
    sw = bs_re.shape[-1]
    nstate = N_GROUPS * SSM_STATE
    proj3 = proj.reshape(bsz, seq, N_MAIN)
    u_spec = pl.BlockSpec((1, seq, LANES), lambda b, j: (b, 0, j))
    st_spec = pl.BlockSpec((ncs, sw), lambda b, j: (b, j))
    par = lambda shape: pl.BlockSpec((1, *shape), lambda b, j: (j, 0, 0, 0))
    st_shape = jax.ShapeDtypeStruct((bsz * ncs, nstate), F32)

    s_re, s_im = pl.pallas_call(
        _s5_state_kernel,
        grid=(bsz, ntile),
        in_specs=[u_spec, par(bs_re.shape[1:]), par(bs_im.shape[1:])],
        out_specs=[st_spec, st_spec],
        out_shape=[st_shape, st_shape],
        compiler_params=_params("parallel", "parallel"),
        name="s5_state",
    )(proj3, bs_re, bs_im)

    lb = 1024
    scan_spec = pl.BlockSpec((ncs, lb), lambda b, i: (b, i))
    coef_spec = pl.BlockSpec((1, lb), lambda b, i: (0, i))
    h_re, h_im = pl.pallas_call(
        _s5_scan_kernel,
        grid=(bsz, nstate // lb),
        in_specs=[scan_spec, scan_spec, coef_spec, coef_spec],
        out_specs=[scan_spec, scan_spec],
        out_shape=[st_shape, st_shape],
        compiler_params=_params("parallel", "parallel"),
        name="s5_scan",
    )(s_re, s_im, a_re, a_im)

    y = pl.pallas_call(
        _s5_out_kernel,
        grid=(bsz, ntile),
        in_specs=[u_spec, st_spec, st_spec, par(bk.shape[1:]), par(mo_re.shape[1:]),
                  par(mo_im.shape[1:]), pl.BlockSpec((1, LANES), lambda b, j: (0, j))],
        out_specs=u_spec,
        out_shape=jax.ShapeDtypeStruct((bsz, seq, D_SSM), BF16),
        scratch_shapes=[pltpu.VMEM((seq, LANES), F32)],
        compiler_params=_params("parallel", "parallel"),
        name="s5_out",
    )(proj3, h_re, h_im, bk, mo_re, mo_im, dvec).reshape(t, D_SSM)

    tm, tn = min(512, t), 512
    nj = D_SSM // tn
    return pl.pallas_call(
        _glu_kernel,
        grid=(t // tm, nj),
        in_specs=[pl.BlockSpec((tm, D_SSM), lambda i, j: (i, 0)),
                  pl.BlockSpec((D_SSM, tn), lambda i, j: (0, j)),
                  pl.BlockSpec((D_SSM, tn), lambda i, j: (0, j + nj))],
        out_specs=pl.BlockSpec((tm, tn), lambda i, j: (i, j)),
        out_shape=jax.ShapeDtypeStruct((t, D_SSM), BF16),
        compiler_params=_params("parallel", "parallel"),
        name="s5_glu",
    )(y, w_glu, w_glu)


def _dn_conv_kernel(x_ref, w_ref, o_ref):
    j = pl.program_id(1)
    x = x_ref[0]
    w = w_ref[...]
    row = lax.broadcasted_iota(jnp.int32, x.shape, 0)
    acc = x * w[CONV_WIDTH - 1:CONV_WIDTH, :]
    for k in range(CONV_WIDTH - 1):
        sh = CONV_WIDTH - 1 - k
        xs = jnp.where(row >= sh, pltpu.roll(x, sh, axis=0), 0.0)
        acc = acc + xs * w[k:k + 1, :]
    y = acc * _sigmoid(acc)
    nrm = y * lax.rsqrt(jnp.sum(y * y, axis=-1, keepdims=True) + NORM_EPS)
    qscale = jnp.where(j < N_HEADS, HEAD_DIM ** -0.5, 1.0).astype(F32)
    o_ref[0] = jnp.where(j < 2 * N_HEADS, nrm * qscale, y)


def _gates_kernel(x_ref, alog_ref, dtb_ref, o_ref):
    x = x_ref[...]
    lane = lax.broadcasted_iota(jnp.int32, x.shape, 1)
    xs = x + dtb_ref[...]
    softplus = jnp.maximum(xs, 0.0) + jnp.log1p(jnp.exp(-jnp.abs(xs)))
    o_ref[...] = jnp.where(lane < N_HEADS, _sigmoid(x), -jnp.exp(alog_ref[...]) * softplus)


def _dn_kernel(q_ref, k_ref, v_ref, z_ref, gt_ref, nw_ref, o_ref, st_s, *, nchunks):
    c, dh = DN_CHUNK, HEAD_DIM

    @pl.when(pl.program_id(1) == 0)
    def _():
        st_s[...] = jnp.zeros_like(st_s)

    ri = lax.broadcasted_iota(jnp.int32, (c, c), 0)
    ci = lax.broadcasted_iota(jnp.int32, (c, c), 1)
    tril, strict = ri >= ci, ri > ci
    tril_f = tril.astype(F32)
    triu_f = (ri <= ci).astype(F32)
    eye = (ri == ci).astype(F32)
    bf = lambda x: x.astype(BF16)

    def chunk(n, carry):
        rows = pl.ds(pl.multiple_of(n * c, c), c)
        gt = gt_ref[0, rows, :]
        gcum_c = _dot(tril_f, gt, HI)
        gcum_r = _dot_tn(gt, triu_f, HI)
        heads = range(N_HEADS)
        cols = [slice(h * dh, (h + 1) * dh) for h in heads]
        beta = [gt[:, h:h + 1] for h in heads]
        gc = [gcum_c[:, N_HEADS + h:N_HEADS + h + 1] for h in heads]
        gr = [gcum_r[N_HEADS + h:N_HEADS + h + 1, :] for h in heads]
        gc_b = [jnp.broadcast_to(gc[h], (c, dh)) for h in heads]
        g_last = [gc_b[h][c - 1:c, :] for h in heads]
        decay = [jnp.where(tril, jnp.exp(jnp.where(tril, gc[h] - gr[h], 0.0)), 0.0) for h in heads]
        q = [q_ref[0, rows, cols[h]] for h in heads]
        k = [k_ref[0, rows, cols[h]] for h in heads]
        kb = [k[h] * beta[h] for h in heads]
        k16 = [bf(k[h]) for h in heads]
        lmat = [jnp.where(strict, _dot_nt(bf(kb[h]), k16[h]) * decay[h], 0.0) for h in heads]
        attn = [bf(jnp.where(tril, _dot_nt(bf(q[h]), k16[h]) * decay[h], 0.0)) for h in heads]
        tmat = [eye - lmat[h] for h in heads]
        lp16 = [bf(lmat[h]) for h in heads]
        lp16 = [bf(_dot(lp16[h], lp16[h])) for h in heads]
        for step in range(5):
            tmat = [tmat[h] + _dot(bf(tmat[h]), lp16[h]) for h in heads]
            if step < 4:
                lp16 = [bf(_dot(lp16[h], lp16[h])) for h in heads]
        t16 = [bf(tmat[h]) for h in heads]
        egc = [jnp.exp(gc_b[h]) for h in heads]
        w_val = [_dot(t16[h], bf(v_ref[0, rows, cols[h]] * beta[h])) for h in heads]
        k_cum = [bf(_dot(t16[h], bf(kb[h] * egc[h]))) for h in heads]
        state = [st_s[h] for h in heads]
        s16 = [bf(state[h]) for h in heads]
        out = [_dot(bf(q[h] * egc[h]), s16[h]) for h in heads]
        vn16 = [bf(w_val[h] - _dot(k_cum[h], s16[h])) for h in heads]
        out = [out[h] + _dot(attn[h], vn16[h]) for h in heads]
        for h in heads:
            k_tail = bf(k[h] * jnp.exp(g_last[h] - gc_b[h]))
            st_s[h] = state[h] * jnp.exp(g_last[h]) + _dot_tn(k_tail, vn16[h])
        for h in heads:
            o = out[h] * lax.rsqrt(jnp.mean(out[h] * out[h], axis=-1, keepdims=True) + NORM_EPS)
            z = z_ref[0, rows, cols[h]]
            o_ref[0, rows, cols[h]] = (o * nw_ref[...] * (z * _sigmoid(z))).astype(o_ref.dtype)
        return carry

    lax.fori_loop(0, nchunks, chunk, 0)


def _dn_mixer(proj, gates_raw, bsz, seq, conv_w, a_log, dt_bias, norm_w):
    t = bsz * seq
    proj3 = proj.reshape(bsz, seq, N_MAIN)
    qkv_off = D_SSM // LANES
    n_tiles = 3 * D_DN // LANES
    qkv = pl.pallas_call(
        _dn_conv_kernel,
        grid=(bsz, n_tiles),
        in_specs=[pl.BlockSpec((1, seq, LANES), lambda b, j: (b, 0, j + qkv_off)),
                  pl.BlockSpec((CONV_WIDTH, LANES), lambda b, j: (0, j))],
        out_specs=pl.BlockSpec((1, seq, LANES), lambda b, j: (b, 0, j)),
        out_shape=jax.ShapeDtypeStruct((bsz, seq, 3 * D_DN), F32),
        compiler_params=_params("parallel", "parallel"),
        name="dn_conv",
    )(proj3, conv_w)

    pad = jnp.zeros((1, LANES), F32)
    alog_p = pad.at[0, N_HEADS:2 * N_HEADS].set(a_log)
    dtb_p = pad.at[0, N_HEADS:2 * N_HEADS].set(dt_bias)
    tm = min(1024, t)
    gates = pl.pallas_call(
        _gates_kernel,
        grid=(t // tm,),
        in_specs=[pl.BlockSpec((tm, LANES), lambda i: (i, 0)),
                  pl.BlockSpec((1, LANES), lambda i: (0, 0)),
                  pl.BlockSpec((1, LANES), lambda i: (0, 0))],
        out_specs=pl.BlockSpec((tm, LANES), lambda i: (i, 0)),
        out_shape=jax.ShapeDtypeStruct((t, LANES), F32),
        compiler_params=_params("parallel"),
        name="dn_gates",
    )(gates_raw, alog_p, dtb_p).reshape(bsz, seq, LANES)

    tb = min(512, seq)
    part = lambda off: pl.BlockSpec((1, tb, D_DN), lambda b, i: (b, i, off))
    return pl.pallas_call(
        functools.partial(_dn_kernel, nchunks=tb // DN_CHUNK),
        grid=(bsz, seq // tb),
        in_specs=[part(0), part(1), part(2), part((D_SSM + 3 * D_DN) // D_DN),
                  pl.BlockSpec((1, tb, LANES), lambda b, i: (b, i, 0)),
                  pl.BlockSpec((1, HEAD_DIM), lambda b, i: (0, 0))],
        out_specs=pl.BlockSpec((1, tb, D_DN), lambda b, i: (b, i, 0)),
        out_shape=jax.ShapeDtypeStruct((bsz, seq, D_DN), BF16),
        scratch_shapes=[pltpu.VMEM((N_HEADS, HEAD_DIM, HEAD_DIM), F32)],
        compiler_params=_params("parallel", "arbitrary"),
        name="dn_chunk",
    )(qkv, qkv, qkv, proj3, gates, norm_w.reshape(1, HEAD_DIM)).reshape(t, D_DN)


def _out_ln_kernel(a1_ref, a2_ref, w1_ref, w2_ref, x_ref, gt_ref, g_ref, b_ref, o_ref, *, alpha):
    y = _dot(a1_ref[...], w1_ref[...]) + _dot(a2_ref[...], w2_ref[...])
    r = alpha * x_ref[...] + gt_ref[0] * y
    o_ref[...] = _ln(r) * g_ref[...] + b_ref[...]


def _out_ln(y_ssm, y_dn, w_out, x, gt, gain, bias, seq, alpha):
    t, d = x.shape
    tm = min(256, seq)
    per = seq // tm
    return pl.pallas_call(
        functools.partial(_out_ln_kernel, alpha=alpha),
        grid=(t // tm,),
        in_specs=[pl.BlockSpec((tm, D_SSM), lambda i: (i, 0)),
                  pl.BlockSpec((tm, D_DN), lambda i: (i, 0)),
                  pl.BlockSpec((D_SSM, d), lambda i: (0, 0)),
                  pl.BlockSpec((D_DN, d), lambda i: (1, 0)),
                  pl.BlockSpec((tm, d), lambda i: (i, 0)),
                  pl.BlockSpec((1, 1, d), lambda i: (i // per, 0, 0)),
                  pl.BlockSpec((1, d), lambda i: (0, 0)),
                  pl.BlockSpec((1, d), lambda i: (0, 0))],
        out_specs=pl.BlockSpec((tm, d), lambda i: (i, 0)),
        out_shape=jax.ShapeDtypeStruct((t, d), F32),
        compiler_params=_params("parallel"),
        name="out_ln",
    )(y_ssm, y_dn, w_out, w_out, x, gt, gain.reshape(1, d), bias.reshape(1, d))


def _top_values(s, count):
    vals = []
    for r in range(count):
        m = jnp.max(s, axis=0, keepdims=True)
        vals.append(m)
        if r + 1 < count:
            s = jnp.where(s == m, -jnp.inf, s)
    return vals


def _peer_route_kernel(ht_ref, wq_ref, keys_ref, a_ref, cnt_ref, b_ref, rank_ref, q_s):
    q_s[...] = _dot(wq_ref[...], ht_ref[...])
    k = PEER_TOPK

    def head(h, carry):
        base = pl.multiple_of(h * 2 * PEER_NKEYS, 2 * PEER_NKEYS)
        s0 = _dot(keys_ref[h, 0], q_s[pl.ds(base, PEER_NKEYS), :], HI)
        s1 = _dot(keys_ref[h, 1], q_s[pl.ds(base + PEER_NKEYS, PEER_NKEYS), :], HI)
        v0, v1 = _top_values(s0, k), _top_values(s1, k)
        v1m = jnp.concatenate(v1, axis=0)
        cands = [v0[i] + v1m for i in range(k)]
        top = v0[0] + v1[0]
        z = jnp.zeros_like(top)
        thr = top
        for r in range(k):
            m = cands[0]
            for cnd in cands[1:]:
                m = jnp.maximum(m, cnd)
            m = jnp.max(m, axis=0, keepdims=True)
            z = z + jnp.exp(m - top)
            thr = m
            if r + 1 < k:
                cands = [jnp.where(cnd == m, -jnp.inf, cnd) for cnd in cands]
        cnt = jnp.zeros_like(s0)
        rank = jnp.zeros_like(s1)
        for r in range(k):
            cnt = cnt + jnp.where(s0 + v1[r] >= thr, 1.0, 0.0)
            rank = rank + jnp.where(v1[r] > s1, 1.0, 0.0)
        a_ref[h] = jnp.exp(s0 - v0[0]) / z
        b_ref[h] = jnp.exp(s1 - v1[0]).astype(b_ref.dtype)
        cnt_ref[h] = cnt
        rank_ref[h] = rank.astype(rank_ref.dtype)
        return carry

    lax.fori_loop(0, PEER_HEADS, head, 0)


def _peer_dense_kernel(ht_ref, u_ref, vt_ref, a_ref, cnt_ref, b_ref, rank_ref, o_ref,
                       act_s, p_s, *, tl):
    e = pl.program_id(1)

    @pl.when(e == 0)
    def _():
        o_ref[...] = jnp.zeros_like(o_ref)

    first = pl.ds(pl.multiple_of(e * PEER_SUB, PEER_SUB), PEER_SUB)
    grp = 2 * PEER_NKEYS
    npieces = PEER_TE // grp

    def up(pc):
        act_s[pc % 2] = _dot(u_ref[pc * grp:(pc + 1) * grp, :], ht_ref[...])

    def down(pc):
        o_ref[...] += _dot(vt_ref[:, pc * grp:(pc + 1) * grp], p_s[pc % 2])

    def gate(pc):
        for lc in range(tl // LANES):
            ls = slice(lc * LANES, (lc + 1) * LANES)
            for j in range(2):
                ii = 2 * pc + j
                er = slice(j * PEER_NKEYS, (j + 1) * PEER_NKEYS)
                w = jnp.zeros((PEER_NKEYS, LANES), BF16)
                for h in range(PEER_HEADS):
                    row = lambda ref: jnp.broadcast_to(ref[h, first, ls][ii:ii + 1, :],
                                                       (PEER_NKEYS, LANES)).astype(BF16)
                    keep = rank_ref[h, :, ls] < row(cnt_ref)
                    w = w + jnp.where(keep, b_ref[h, :, ls], jnp.zeros((), BF16)) * row(a_ref)
                g = _gelu(act_s[pc % 2, er, ls]).astype(BF16)
                p_s[pc % 2, er, ls] = g * w

    live = e >= 0
    pl.when(live)(lambda: up(0))
    for pc in range(npieces):
        def stage(pc=pc):
            if pc + 1 < npieces:
                up(pc + 1)
            if pc >= 1:
                down(pc - 1)
            gate(pc)
        pl.when(live)(stage)
    pl.when(live)(lambda: down(npieces - 1))


def _peer_ffn(h_t, wq_t, keys, u_bf, vt_bf):
    d, t = h_t.shape
    nq = wq_t.shape[0]
    tl = min(256, t)
    gate = pl.BlockSpec((PEER_HEADS, PEER_NKEYS, tl), lambda i: (0, 0, i))
    gshape = lambda dt: jax.ShapeDtypeStruct((PEER_HEADS, PEER_NKEYS, t), dt)
    a, cnt, b, rank = pl.pallas_call(
        _peer_route_kernel,
        grid=(t // tl,),
        in_specs=[pl.BlockSpec((d, tl), lambda i: (0, i)),
                  pl.BlockSpec((nq, d), lambda i: (0, 0)),
                  pl.BlockSpec((PEER_HEADS, 2, PEER_NKEYS, PEER_NKEYS), lambda i: (0, 0, 0, 0))],
        out_specs=[gate, gate, gate, gate],
        out_shape=[gshape(F32), gshape(F32), gshape(BF16), gshape(BF16)],
        scratch_shapes=[pltpu.VMEM((nq, tl), F32)],
        compiler_params=_params("parallel"),
        name="peer_route",
    )(h_t, wq_t, keys)

    tl, te = min(512, t), PEER_TE
    gate = pl.BlockSpec((PEER_HEADS, PEER_NKEYS, tl), lambda i, e: (0, 0, i))
    return pl.pallas_call(
        functools.partial(_peer_dense_kernel, tl=tl),
        grid=(t // tl, PEER_EXPERTS // te),
        in_specs=[pl.BlockSpec((d, tl), lambda i, e: (0, i)),
                  pl.BlockSpec((te, d), lambda i, e: (e, 0)),
                  pl.BlockSpec((d, te), lambda i, e: (0, e)),
                  gate, gate, gate, gate],
        out_specs=pl.BlockSpec((d, tl), lambda i, e: (0, i)),
        out_shape=jax.ShapeDtypeStruct((d, t), F32),
        scratch_shapes=[pltpu.VMEM((2, 2 * PEER_NKEYS, tl), F32),
                        pltpu.VMEM((2, 2 * PEER_NKEYS, tl), BF16)],
        compiler_params=_params("parallel", "arbitrary"),
        name="peer_dense",
    )(h_t, u_bf, vt_bf, a, cnt, b, rank)


def kernel(x, c, w_ada, b_ada, w_in, ssm_lam_re, ssm_lam_im, ssm_log_step, ssm_b_re, ssm_b_im, ssm_c_re, ssm_c_im, ssm_d, ssm_w_glu, dn_conv_w, dn_a_log, dn_dt_bias, dn_norm_w, w_out, ln1_g, ln1_b, peer_w_query, peer_sub_keys, peer_u, peer_v, ln2_g, ln2_b):
    bsz, seq, d = x.shape
    depth = w_ada.shape[0]
    t = bsz * seq
    alpha = (2.0 * depth) ** 0.25

    mod = _ada_mod(c, w_ada, b_ada)[:, :bsz]
    xf = x.reshape(t, d)
    for l in range(depth):
        sh1, sc1, gt1, sh2, sc2, gt2 = [m.reshape(bsz, 1, d) for m in jnp.split(mod[l], 6, axis=-1)]

        hmix = _ln_mod(xf, sh1, sc1, seq)
        w_in_bf = w_in[l].astype(BF16)
        proj = _mm(hmix, w_in_bf, F32, 512, 1024, "in_proj", n=N_MAIN)
        w_gate = jnp.zeros((d, LANES), BF16).at[:, :2 * N_HEADS].set(w_in_bf[:, N_MAIN:])
        gates_raw = _mm(hmix, w_gate, F32, 512, LANES, "gate_proj")

        prep = _s5_prep(ssm_lam_re[l], ssm_lam_im[l], ssm_log_step[l], ssm_b_re[l], ssm_b_im[l],
                        ssm_c_re[l], ssm_c_im[l], ssm_d[l])
        y_ssm = _s5_mixer(proj, bsz, seq, prep, ssm_w_glu[l].astype(BF16))
        y_dn = _dn_mixer(proj, gates_raw, bsz, seq, dn_conv_w[l], dn_a_log[l], dn_dt_bias[l],
                         dn_norm_w[l])
        xf = _out_ln(y_ssm, y_dn, w_out[l].astype(BF16), xf, gt1, ln1_g[l], ln1_b[l], seq, alpha)

        hffn_t = _ln_mod(xf, sh2, sc2, seq, feature_major=True)
        y_t = _peer_ffn(hffn_t, peer_w_query[l].T.astype(BF16), peer_sub_keys[l],
                        peer_u[l].astype(BF16), peer_v[l].T.astype(BF16))
        xf = _res_ln(xf, y_t, gt2, ln2_g[l], ln2_b[l], seq, alpha)
    return xf.reshape(bsz, seq, d)
```

```python
import functools
import math

import jax
import jax.numpy as jnp
from jax import lax
from jax.experimental import pallas as pl
from jax.experimental.pallas import tpu as pltpu

F32 = jnp.float32
BF16 = jnp.bfloat16
HI = lax.Precision.HIGHEST

D_MODEL = 2048
D_SSM = 1024
SSM_GROUP = 16
N_GROUPS = D_SSM // SSM_GROUP
SSM_STATE = 64
SSM_CHUNK = 16
N_HEADS = 8
HEAD_DIM = 128
D_DN = N_HEADS * HEAD_DIM
DN_CHUNK = 64
CONV_WIDTH = 4
N_MAIN = D_SSM + 4 * D_DN
PEER_HEADS = 8
PEER_NKEYS = 128
PEER_EXPERTS = PEER_NKEYS * PEER_NKEYS
PEER_TOPK = 16
PEER_SUB = 8
PEER_TE = PEER_SUB * PEER_NKEYS
PEER_PIECE = 4
LN_EPS = 1e-5
NORM_EPS = 1e-6

LANES = 128
VMEM_LIMIT = 56 * 1024 * 1024


def _params(*sem):
    return pltpu.CompilerParams(dimension_semantics=sem, vmem_limit_bytes=VMEM_LIMIT)


def _gelu(x):
    return 0.5 * x * (1.0 + jnp.tanh(math.sqrt(2.0 / math.pi) * (x + 0.044715 * (x * x * x))))


def _sigmoid(x):
    return 1.0 / (1.0 + jnp.exp(-x))


def _dot(a, b, precision=None):
    return jnp.dot(a, b, precision=precision, preferred_element_type=F32)


def _dot_nt(a, b, precision=None):
    return lax.dot_general(a, b, (((1,), (1,)), ((), ())), precision=precision,
                           preferred_element_type=F32)


def _dot_tn(a, b, precision=None):
    return lax.dot_general(a, b, (((0,), (0,)), ((), ())), precision=precision,
                           preferred_element_type=F32)


def _ada_kernel(c_ref, w_ref, b_ref, o_ref):
    c = c_ref[...]
    ca = c * _sigmoid(c)
    hi = ca.astype(BF16)
    lo = (ca - hi.astype(F32)).astype(BF16)
    w = w_ref[0].astype(BF16)
    o_ref[0] = _dot(hi, w) + _dot(lo, w) + b_ref[0]


def _ada_mod(c, w_ada, b_ada):
    depth, d, n = w_ada.shape
    tn = 1024
    cp = jnp.zeros((8, d), F32).at[: c.shape[0]].set(c)
    return pl.pallas_call(
        _ada_kernel,
        grid=(depth, n // tn),
        in_specs=[pl.BlockSpec((8, d), lambda l, j: (0, 0)),
                  pl.BlockSpec((1, d, tn), lambda l, j: (l, 0, j)),
                  pl.BlockSpec((1, 1, tn), lambda l, j: (l, 0, j))],
        out_specs=pl.BlockSpec((1, 8, tn), lambda l, j: (l, 0, j)),
        out_shape=jax.ShapeDtypeStruct((depth, 8, n), F32),
        compiler_params=_params("parallel", "parallel"),
        name="ada_mod",
    )(cp, w_ada, b_ada.reshape(depth, 1, n))


def _ln(x):
    mu = jnp.mean(x, axis=-1, keepdims=True)
    xc = x - mu
    var = jnp.mean(xc * xc, axis=-1, keepdims=True)
    return xc * lax.rsqrt(var + LN_EPS)


def _ln_mod_kernel(x_ref, sh_ref, sc_ref, o_ref, *, feature_major):
    y = _ln(x_ref[...]) * (1.0 + sc_ref[0]) + sh_ref[0]
    o_ref[...] = (y.T if feature_major else y).astype(o_ref.dtype)


def _ln_mod(x, sh, sc, seq, feature_major=False):
    t, d = x.shape
    tm = min(256, seq)
    per = seq // tm
    if feature_major:
        out_spec, out_shape = pl.BlockSpec((d, tm), lambda i: (0, i)), (d, t)
    else:
        out_spec, out_shape = pl.BlockSpec((tm, d), lambda i: (i, 0)), (t, d)
    return pl.pallas_call(
        functools.partial(_ln_mod_kernel, feature_major=feature_major),
        grid=(t // tm,),
        in_specs=[pl.BlockSpec((tm, d), lambda i: (i, 0)),
                  pl.BlockSpec((1, 1, d), lambda i: (i // per, 0, 0)),
                  pl.BlockSpec((1, 1, d), lambda i: (i // per, 0, 0))],
        out_specs=out_spec,
        out_shape=jax.ShapeDtypeStruct(out_shape, BF16),
        compiler_params=_params("parallel"),
        name="ln_mod",
    )(x, sh, sc)


def _res_ln_kernel(x_ref, yt_ref, gt_ref, g_ref, b_ref, o_ref, *, alpha):
    r = alpha * x_ref[...] + gt_ref[0] * yt_ref[...].T
    o_ref[...] = _ln(r) * g_ref[...] + b_ref[...]


def _res_ln(x, y_t, gt, gain, bias, seq, alpha):
    t, d = x.shape
    tm = min(256, seq)
    per = seq // tm
    return pl.pallas_call(
        functools.partial(_res_ln_kernel, alpha=alpha),
        grid=(t // tm,),
        in_specs=[pl.BlockSpec((tm, d), lambda i: (i, 0)),
                  pl.BlockSpec((d, tm), lambda i: (0, i)),
                  pl.BlockSpec((1, 1, d), lambda i: (i // per, 0, 0)),
                  pl.BlockSpec((1, d), lambda i: (0, 0)),
                  pl.BlockSpec((1, d), lambda i: (0, 0))],
        out_specs=pl.BlockSpec((tm, d), lambda i: (i, 0)),
        out_shape=jax.ShapeDtypeStruct((t, d), F32),
        compiler_params=_params("parallel"),
        name="res_ln",
    )(x, y_t, gt, gain.reshape(1, d), bias.reshape(1, d))


def _mm_kernel(a_ref, b_ref, o_ref):
    o_ref[...] = _dot(a_ref[...], b_ref[...]).astype(o_ref.dtype)


def _mm(a, b, layer, out_dtype, tm, tn, name, n=None):
    m, k = a.shape
    n = b.shape[2] if n is None else n
    tm, tn = min(tm, m), min(tn, n)
    return pl.pallas_call(
        _mm_kernel,
        grid=(m // tm, n // tn),
        in_specs=[pl.BlockSpec((tm, k), lambda i, j: (i, 0)),
                  pl.BlockSpec((None, k, tn), lambda i, j: (layer, 0, j))],
        out_specs=pl.BlockSpec((tm, tn), lambda i, j: (i, j)),
        out_shape=jax.ShapeDtypeStruct((m, n), out_dtype),
        compiler_params=_params("parallel", "parallel"),
        name=name,
    )(a, b)


def _s5_prep(lam_re, lam_im, log_step, b_re, b_im, c_re, c_im, d_skip):
    g, p, nch = N_GROUPS, SSM_STATE, SSM_CHUNK
    gpt = LANES // SSM_GROUP
    ntile = g // gpt
    step = jnp.exp(log_step)[:, None]
    zr, zi = lam_re * step, lam_im * step
    ks = jnp.arange(nch + 1, dtype=F32)[:, None, None]
    mag = jnp.exp(ks * zr)
    pr, pi = mag * jnp.cos(ks * zi), mag * jnp.sin(ks * zi)
    nr, ni = pr[1] - 1.0, pi[1]
    den = lam_re * lam_re + lam_im * lam_im
    fr = (nr * lam_re + ni * lam_im) / den
    fi = (ni * lam_re - nr * lam_im) / den
    bbr = fr[..., None] * b_re - fi[..., None] * b_im
    bbi = fr[..., None] * b_im + fi[..., None] * b_re
    er = pr[:nch, :, :, None] * bbr - pi[:nch, :, :, None] * bbi
    ei = pr[:nch, :, :, None] * bbi + pi[:nch, :, :, None] * bbr
    kk = (jnp.einsum('gap,kgpc->kgac', c_re, er, precision=HI)
          - jnp.einsum('gap,kgpc->kgac', c_im, ei, precision=HI))
    wr = c_re[None] * pr[1:, :, None, :] - c_im[None] * pi[1:, :, None, :]
    wi = c_re[None] * pi[1:, :, None, :] + c_im[None] * pr[1:, :, None, :]
    eye = jnp.eye(gpt, dtype=F32)[:, None, :, None]

    def blockdiag(m):
        m = m.swapaxes(2, 3)
        rows, cols = m.shape[2:]
        m = m.reshape(nch, ntile, gpt, rows, 1, cols) * eye
        return m.reshape(nch, ntile, gpt * rows, gpt * cols).astype(BF16)

    bs_re, bs_im = blockdiag(er[::-1]), blockdiag(ei[::-1])
    bk = blockdiag(kk).reshape(nch // 2, 2, ntile, LANES, LANES)
    mo_re, mo_im = blockdiag(wr), blockdiag(-wi)
    a_re, a_im = pr[nch].reshape(1, g * p), pi[nch].reshape(1, g * p)
    return bs_re, bs_im, bk, mo_re, mo_im, a_re, a_im, d_skip.reshape(1, D_SSM)


def _s5_state_kernel(u_ref, bsr_ref, bsi_ref, sr_ref, si_ref):
    ncs = sr_ref.shape[0]
    acc_r = jnp.zeros(sr_ref.shape, F32)
    acc_i = jnp.zeros(si_ref.shape, F32)
    for s in range(SSM_CHUNK):
        xs = u_ref[0, pl.ds(s, ncs, stride=SSM_CHUNK), :].astype(BF16)
        acc_r = acc_r + _dot(xs, bsr_ref[s])
        acc_i = acc_i + _dot(xs, bsi_ref[s])
    sr_ref[...] = acc_r
    si_ref[...] = acc_i


def _s5_scan_kernel(sr_ref, si_ref, ar_ref, ai_ref, hr_ref, hi_ref):
    ar, ai = ar_ref[...], ai_ref[...]
    sub = 8

    def group(i, carry):
        hr, hi = carry
        rows = pl.ds(pl.multiple_of(i * sub, sub), sub)
        sr, si = sr_ref[rows, :], si_ref[rows, :]
        out_r, out_i = [], []
        for r in range(sub):
            out_r.append(hr)
            out_i.append(hi)
            hr, hi = ar * hr - ai * hi + sr[r:r + 1, :], ar * hi + ai * hr + si[r:r + 1, :]
        hr_ref[rows, :] = jnp.concatenate(out_r, axis=0)
        hi_ref[rows, :] = jnp.concatenate(out_i, axis=0)
        return hr, hi

    zero = jnp.zeros(ar.shape, F32)
    lax.fori_loop(0, sr_ref.shape[0] // sub, group, (zero, zero))


def _s5_out_kernel(u_ref, hr_ref, hi_ref, bk_ref, mor_ref, moi_ref, d_ref, o_ref, acc_s):
    x = u_ref[0]
    ncs = hr_ref.shape[0]
    pos = lax.broadcasted_iota(jnp.int32, x.shape, 0) % SSM_CHUNK

    def lagged(k):
        if k == 0:
            return x.astype(BF16)
        return jnp.where(pos >= k, pltpu.roll(x, k, axis=0), 0.0).astype(BF16)

    acc = d_ref[...] * x
    for kp in range(SSM_CHUNK // 2):
        xx = jnp.concatenate([lagged(2 * kp), lagged(2 * kp + 1)], axis=1)
        acc = acc + _dot(xx, bk_ref[kp].reshape(2 * LANES, LANES))
    acc_s[...] = acc
    hr, hi = hr_ref[...].astype(BF16), hi_ref[...].astype(BF16)
    for r in range(SSM_CHUNK):
        rows = pl.ds(r, ncs, stride=SSM_CHUNK)
        acc_s[rows, :] = acc_s[rows, :] + _dot(hr, mor_ref[r]) + _dot(hi, moi_ref[r])
    o_ref[0] = _gelu(acc_s[...]).astype(o_ref.dtype)


def _glu_kernel(y_ref, wa_ref, wb_ref, o_ref):
    y = y_ref[...]
    o_ref[...] = (_dot(y, wa_ref[...]) * _sigmoid(_dot(y, wb_ref[...]))).astype(o_ref.dtype)


def _s5_mixer(proj, bsz, seq, prep, w_glu):
    bs_re, bs_im, bk, mo_re, mo_im, a_re, a_im, dvec = prep
    t = bsz * seq
    ncs = seq // SSM_CHUNK
    ntile = D_SSM // LANES
    sw = bs_re.shape[-1]
    nstate = N_GROUPS * SSM_STATE
    proj3 = proj.reshape(bsz, seq, N_MAIN)
    u_spec = pl.BlockSpec((1, seq, LANES), lambda b, j: (b, 0, j))
    st_spec = pl.BlockSpec((ncs, sw), lambda b, j: (b, j))
    par = lambda m: pl.BlockSpec((m.shape[0], None, *m.shape[2:]), lambda b, j: (0, j, 0, 0))
    st_shape = jax.ShapeDtypeStruct((bsz * ncs, nstate), F32)

    s_re, s_im = pl.pallas_call(
        _s5_state_kernel,
        grid=(bsz, ntile),
        in_specs=[u_spec, par(bs_re), par(bs_im)],
        out_specs=[st_spec, st_spec],
        out_shape=[st_shape, st_shape],
        compiler_params=_params("parallel", "parallel"),
        name="s5_state",
    )(proj3, bs_re, bs_im)

    lb = 1024
    scan_spec = pl.BlockSpec((ncs, lb), lambda b, i: (b, i))
    coef_spec = pl.BlockSpec((1, lb), lambda b, i: (0, i))
    h_re, h_im = pl.pallas_call(
        _s5_scan_kernel,
        grid=(bsz, nstate // lb),
        in_specs=[scan_spec, scan_spec, coef_spec, coef_spec],
        out_specs=[scan_spec, scan_spec],
        out_shape=[st_shape, st_shape],
        compiler_params=_params("parallel", "parallel"),
        name="s5_scan",
    )(s_re, s_im, a_re, a_im)

    y = pl.pallas_call(
        _s5_out_kernel,
        grid=(bsz, ntile),
        in_specs=[u_spec, st_spec, st_spec,
                  pl.BlockSpec((*bk.shape[:2], None, LANES, LANES), lambda b, j: (0, 0, j, 0, 0)),
                  par(mo_re), par(mo_im), pl.BlockSpec((1, LANES), lambda b, j: (0, j))],
        out_specs=u_spec,
        out_shape=jax.ShapeDtypeStruct((bsz, seq, D_SSM), BF16),
        scratch_shapes=[pltpu.VMEM((seq, LANES), F32)],
        compiler_params=_params("parallel", "parallel"),
        name="s5_out",
    )(proj3, h_re, h_im, bk, mo_re, mo_im, dvec).reshape(t, D_SSM)

    tm, tn = min(512, t), 512
    nj = D_SSM // tn
    return pl.pallas_call(
        _glu_kernel,
        grid=(t // tm, nj),
        in_specs=[pl.BlockSpec((tm, D_SSM), lambda i, j: (i, 0)),
                  pl.BlockSpec((D_SSM, tn), lambda i, j: (0, j)),
                  pl.BlockSpec((D_SSM, tn), lambda i, j: (0, j + nj))],
        out_specs=pl.BlockSpec((tm, tn), lambda i, j: (i, j)),
        out_shape=jax.ShapeDtypeStruct((t, D_SSM), BF16),
        compiler_params=_params("parallel", "parallel"),
        name="s5_glu",
    )(y, w_glu, w_glu)


def _dn_conv_kernel(x_ref, w_ref, o_ref):
    j = pl.program_id(1)
    x = x_ref[0]
    w = w_ref[...]
    row = lax.broadcasted_iota(jnp.int32, x.shape, 0)
    acc = x * w[CONV_WIDTH - 1:CONV_WIDTH, :]
    for k in range(CONV_WIDTH - 1):
        sh = CONV_WIDTH - 1 - k
        xs = jnp.where(row >= sh, pltpu.roll(x, sh, axis=0), 0.0)
        acc = acc + xs * w[k:k + 1, :]
    y = acc * _sigmoid(acc)
    nrm = y * lax.rsqrt(jnp.sum(y * y, axis=-1, keepdims=True) + NORM_EPS)
    qscale = jnp.where(j < N_HEADS, HEAD_DIM ** -0.5, 1.0).astype(F32)
    o_ref[0] = jnp.where(j < 2 * N_HEADS, nrm * qscale, y)


def _gates_kernel(x_ref, alog_ref, dtb_ref, o_ref):
    x = x_ref[...]
    lane = lax.broadcasted_iota(jnp.int32, x.shape, 1)
    xs = x + dtb_ref[...]
    softplus = jnp.maximum(xs, 0.0) + jnp.log1p(jnp.exp(-jnp.abs(xs)))
    o_ref[...] = jnp.where(lane < N_HEADS, _sigmoid(x), -jnp.exp(alog_ref[...]) * softplus)


def _dn_kernel(q_ref, k_ref, v_ref, z_ref, gt_ref, nw_ref, o_ref, st_s, *, nchunks):
    c, dh = DN_CHUNK, HEAD_DIM

    @pl.when(pl.program_id(1) == 0)
    def _():
        st_s[...] = jnp.zeros_like(st_s)

    ri = lax.broadcasted_iota(jnp.int32, (c, c), 0)
    ci = lax.broadcasted_iota(jnp.int32, (c, c), 1)
    tril, strict = ri >= ci, ri > ci
    tril_f = tril.astype(F32)
    triu_f = (ri <= ci).astype(F32)
    eye = (ri == ci).astype(F32)
    bf = lambda x: x.astype(BF16)

    def chunk(n, carry):
        rows = pl.ds(pl.multiple_of(n * c, c), c)
        gt = gt_ref[0, rows, :]
        gcum_c = _dot(tril_f, gt, HI)
        gcum_r = _dot_tn(gt, triu_f, HI)
        heads = range(N_HEADS)
        cols = [slice(h * dh, (h + 1) * dh) for h in heads]
        beta = [gt[:, h:h + 1] for h in heads]
        gc = [gcum_c[:, N_HEADS + h:N_HEADS + h + 1] for h in heads]
        gr = [gcum_r[N_HEADS + h:N_HEADS + h + 1, :] for h in heads]
        gc_b = [jnp.broadcast_to(gc[h], (c, dh)) for h in heads]
        g_last = [gc_b[h][c - 1:c, :] for h in heads]
        decay = [jnp.where(tril, jnp.exp(jnp.where(tril, gc[h] - gr[h], 0.0)), 0.0) for h in heads]
        q = [q_ref[0, rows, cols[h]] for h in heads]
        k = [k_ref[0, rows, cols[h]] for h in heads]
        kb = [k[h] * beta[h] for h in heads]
        k16 = [bf(k[h]) for h in heads]
        lmat = [jnp.where(strict, _dot_nt(bf(kb[h]), k16[h]) * decay[h], 0.0) for h in heads]
        attn = [bf(jnp.where(tril, _dot_nt(bf(q[h]), k16[h]) * decay[h], 0.0)) for h in heads]
        tmat = [eye - lmat[h] for h in heads]
        lp16 = [bf(lmat[h]) for h in heads]
        lp16 = [bf(_dot(lp16[h], lp16[h])) for h in heads]
        for step in range(5):
            tmat = [tmat[h] + _dot(bf(tmat[h]), lp16[h]) for h in heads]
            if step < 4:
                lp16 = [bf(_dot(lp16[h], lp16[h])) for h in heads]
        t16 = [bf(tmat[h]) for h in heads]
        egc = [jnp.exp(gc_b[h]) for h in heads]
        w_val = [_dot(t16[h], bf(v_ref[0, rows, cols[h]] * beta[h])) for h in heads]
        k_cum = [bf(_dot(t16[h], bf(kb[h] * egc[h]))) for h in heads]
        state = [st_s[h] for h in heads]
        s16 = [bf(state[h]) for h in heads]
        out = [_dot(bf(q[h] * egc[h]), s16[h]) for h in heads]
        vn16 = [bf(w_val[h] - _dot(k_cum[h], s16[h])) for h in heads]
        out = [out[h] + _dot(attn[h], vn16[h]) for h in heads]
        for h in heads:
            k_tail = bf(k[h] * jnp.exp(g_last[h] - gc_b[h]))
            st_s[h] = state[h] * jnp.exp(g_last[h]) + _dot_tn(k_tail, vn16[h])
        for h in heads:
            o = out[h] * lax.rsqrt(jnp.mean(out[h] * out[h], axis=-1, keepdims=True) + NORM_EPS)
            z = z_ref[0, rows, cols[h]]
            o_ref[0, rows, cols[h]] = (o * nw_ref[...] * (z * _sigmoid(z))).astype(o_ref.dtype)
        return carry

    lax.fori_loop(0, nchunks, chunk, 0)


def _dn_mixer(proj, gates_raw, bsz, seq, conv_w, a_log, dt_bias, norm_w):
    t = bsz * seq
    proj3 = proj.reshape(bsz, seq, N_MAIN)
    qkv_off = D_SSM // LANES
    n_tiles = 3 * D_DN // LANES
    qkv = pl.pallas_call(
        _dn_conv_kernel,
        grid=(bsz, n_tiles),
        in_specs=[pl.BlockSpec((1, seq, LANES), lambda b, j: (b, 0, j + qkv_off)),
                  pl.BlockSpec((CONV_WIDTH, LANES), lambda b, j: (0, j))],
        out_specs=pl.BlockSpec((1, seq, LANES), lambda b, j: (b, 0, j)),
        out_shape=jax.ShapeDtypeStruct((bsz, seq, 3 * D_DN), F32),
        compiler_params=_params("parallel", "parallel"),
        name="dn_conv",
    )(proj3, conv_w)

    pad = jnp.zeros((1, LANES), F32)
    alog_p = pad.at[0, N_HEADS:2 * N_HEADS].set(a_log)
    dtb_p = pad.at[0, N_HEADS:2 * N_HEADS].set(dt_bias)
    tm = min(1024, t)
    gates = pl.pallas_call(
        _gates_kernel,
        grid=(t // tm,),
        in_specs=[pl.BlockSpec((tm, LANES), lambda i: (i, 0)),
                  pl.BlockSpec((1, LANES), lambda i: (0, 0)),
                  pl.BlockSpec((1, LANES), lambda i: (0, 0))],
        out_specs=pl.BlockSpec((tm, LANES), lambda i: (i, 0)),
        out_shape=jax.ShapeDtypeStruct((t, LANES), F32),
        compiler_params=_params("parallel"),
        name="dn_gates",
    )(gates_raw, alog_p, dtb_p).reshape(bsz, seq, LANES)

    tb = min(512, seq)
    part = lambda off: pl.BlockSpec((1, tb, D_DN), lambda b, i: (b, i, off))
    return pl.pallas_call(
        functools.partial(_dn_kernel, nchunks=tb // DN_CHUNK),
        grid=(bsz, seq // tb),
        in_specs=[part(0), part(1), part(2), part((D_SSM + 3 * D_DN) // D_DN),
                  pl.BlockSpec((1, tb, LANES), lambda b, i: (b, i, 0)),
                  pl.BlockSpec((1, HEAD_DIM), lambda b, i: (0, 0))],
        out_specs=pl.BlockSpec((1, tb, D_DN), lambda b, i: (b, i, 0)),
        out_shape=jax.ShapeDtypeStruct((bsz, seq, D_DN), BF16),
        scratch_shapes=[pltpu.VMEM((N_HEADS, HEAD_DIM, HEAD_DIM), F32)],
        compiler_params=_params("parallel", "arbitrary"),
        name="dn_chunk",
    )(qkv, qkv, qkv, proj3, gates, norm_w.reshape(1, HEAD_DIM)).reshape(t, D_DN)


def _out_ln_kernel(a1_ref, a2_ref, w1_ref, w2_ref, x_ref, gt_ref, g_ref, b_ref, o_ref, *, alpha):
    y = _dot(a1_ref[...], w1_ref[...]) + _dot(a2_ref[...], w2_ref[...])
    r = alpha * x_ref[...] + gt_ref[0] * y
    o_ref[...] = _ln(r) * g_ref[...] + b_ref[...]


def _out_ln(y_ssm, y_dn, w_out, x, gt, gain, bias, seq, alpha):
    t, d = x.shape
    tm = min(256, seq)
    per = seq // tm
    return pl.pallas_call(
        functools.partial(_out_ln_kernel, alpha=alpha),
        grid=(t // tm,),
        in_specs=[pl.BlockSpec((tm, D_SSM), lambda i: (i, 0)),
                  pl.BlockSpec((tm, D_DN), lambda i: (i, 0)),
                  pl.BlockSpec((D_SSM, d), lambda i: (0, 0)),
                  pl.BlockSpec((D_DN, d), lambda i: (1, 0)),
                  pl.BlockSpec((tm, d), lambda i: (i, 0)),
                  pl.BlockSpec((1, 1, d), lambda i: (i // per, 0, 0)),
                  pl.BlockSpec((1, d), lambda i: (0, 0)),
                  pl.BlockSpec((1, d), lambda i: (0, 0))],
        out_specs=pl.BlockSpec((tm, d), lambda i: (i, 0)),
        out_shape=jax.ShapeDtypeStruct((t, d), F32),
        compiler_params=_params("parallel"),
        name="out_ln",
    )(y_ssm, y_dn, w_out, w_out, x, gt, gain.reshape(1, d), bias.reshape(1, d))


def _top_values(s, count):
    vals = []
    rank = jnp.full(s.shape, float(count), F32)
    for r in range(count):
        m = jnp.max(s, axis=0, keepdims=True)
        vals.append(m)
        hit = s == m
        rank = jnp.where(hit, float(r), rank)
        s = jnp.where(hit, -jnp.inf, s)
    return jnp.concatenate(vals, axis=0), rank


def _peer_route_kernel(ht_ref, wq_ref, keys_ref, a_ref, cnt_ref, b_ref, rank_ref, q_s):
    q_s[...] = _dot(wq_ref[...], ht_ref[...])
    k = PEER_TOPK

    def head(h, carry):
        base = pl.multiple_of(h * 2 * PEER_NKEYS, 2 * PEER_NKEYS)
        s0 = _dot(keys_ref[h, 0], q_s[pl.ds(base, PEER_NKEYS), :], HI)
        s1 = _dot(keys_ref[h, 1], q_s[pl.ds(base + PEER_NKEYS, PEER_NKEYS), :], HI)
        v0, rank0 = _top_values(s0, k)
        v1, rank1 = _top_values(s1, k)
        half = k // 2
        sub = lax.broadcasted_iota(jnp.int32, (half, v0.shape[1]), 0)
        cands = [v0[0:1] + v1[:half], v0[0:1] + v1[half:]]
        for x in range(1, half):
            cands.append(jnp.where(sub < k // (x + 1), v0[x:x + 1] + v1[:half], -jnp.inf))
        cands.append(v0[half:] + v1[0:1])
        top = v0[0:1] + v1[0:1]
        z = jnp.zeros_like(top)
        thr = top
        for r in range(k):
            m = cands[0]
            for cnd in cands[1:]:
                m = jnp.maximum(m, cnd)
            m = jnp.max(m, axis=0, keepdims=True)
            z = z + jnp.exp(m - top)
            thr = m
            if r + 1 < k:
                cands = [jnp.where(cnd == m, -jnp.inf, cnd) for cnd in cands]
        cnt_top = jnp.zeros_like(v0)
        for r in range(k):
            cnt_top = cnt_top + jnp.where(v0 + v1[r:r + 1] >= thr, 1.0, 0.0)
        cnt = jnp.zeros_like(s0)
        for x in range(k):
            cnt = jnp.where(rank0 == float(x), cnt_top[x:x + 1], cnt)
        a_ref[h] = jnp.exp(s0 - v0[0:1]) / z
        b_ref[h] = jnp.exp(s1 - v1[0:1]).astype(b_ref.dtype)
        cnt_ref[h] = cnt
        rank_ref[h] = rank1.astype(rank_ref.dtype)
        return carry

    lax.fori_loop(0, PEER_HEADS, head, 0)


def _peer_dense_kernel(ht_ref, u_ref, vt_ref, a_ref, cnt_ref, b_ref, rank_ref, o_ref,
                       act_s, p_s, *, tl):
    e = pl.program_id(1)

    @pl.when(e == 0)
    def _():
        o_ref[...] = jnp.zeros_like(o_ref)

    first = pl.ds(pl.multiple_of(e * PEER_SUB, PEER_SUB), PEER_SUB)
    grp = PEER_PIECE * PEER_NKEYS
    npieces = PEER_TE // grp

    def up(pc):
        act_s[pc % 2] = _dot(u_ref[pc * grp:(pc + 1) * grp, :], ht_ref[...])

    def down(pc):
        o_ref[...] += _dot(vt_ref[:, pc * grp:(pc + 1) * grp], p_s[pc % 2])

    def gate(pc):
        for lc in range(tl // LANES):
            ls = slice(lc * LANES, (lc + 1) * LANES)
            for j in range(PEER_PIECE):
                ii = PEER_PIECE * pc + j
                er = slice(j * PEER_NKEYS, (j + 1) * PEER_NKEYS)
                w = jnp.zeros((PEER_NKEYS, LANES), BF16)
                for h in range(PEER_HEADS):
                    row = lambda ref: jnp.broadcast_to(ref[h, first, ls][ii:ii + 1, :],
                                                       (PEER_NKEYS, LANES)).astype(BF16)
                    keep = rank_ref[h, :, ls] < row(cnt_ref)
                    w = w + jnp.where(keep, b_ref[h, :, ls], jnp.zeros((), BF16)) * row(a_ref)
                g = _gelu(act_s[pc % 2, er, ls]).astype(BF16)
                p_s[pc % 2, er, ls] = g * w

    up(0)
    for pc in range(npieces):
        if pc + 1 < npieces:
            up(pc + 1)
        if pc >= 1:
            down(pc - 1)
        gate(pc)
    down(npieces - 1)


def _peer_ffn(h_t, layer, wq_t, keys, u_bf, vt_bf):
    d, t = h_t.shape
    nq = wq_t.shape[1]
    tl = min(256, t)
    gate = pl.BlockSpec((PEER_HEADS, PEER_NKEYS, tl), lambda i: (0, 0, i))
    gshape = lambda dt: jax.ShapeDtypeStruct((PEER_HEADS, PEER_NKEYS, t), dt)
    a, cnt, b, rank = pl.pallas_call(
        _peer_route_kernel,
        grid=(t // tl,),
        in_specs=[pl.BlockSpec((d, tl), lambda i: (0, i)),
                  pl.BlockSpec((None, nq, d), lambda i: (layer, 0, 0)),
                  pl.BlockSpec((PEER_HEADS, 2, PEER_NKEYS, PEER_NKEYS), lambda i: (0, 0, 0, 0))],
        out_specs=[gate, gate, gate, gate],
        out_shape=[gshape(F32), gshape(F32), gshape(BF16), gshape(BF16)],
        scratch_shapes=[pltpu.VMEM((nq, tl), F32)],
        compiler_params=_params("parallel"),
        name="peer_route",
    )(h_t, wq_t, keys)

    tl, te = min(512, t), PEER_TE
    gate = pl.BlockSpec((PEER_HEADS, PEER_NKEYS, tl), lambda i, e: (0, 0, i))
    return pl.pallas_call(
        functools.partial(_peer_dense_kernel, tl=tl),
        grid=(t // tl, PEER_EXPERTS // te),
        in_specs=[pl.BlockSpec((d, tl), lambda i, e: (0, i)),
                  pl.BlockSpec((None, te, d), lambda i, e: (layer, e, 0)),
                  pl.BlockSpec((None, d, te), lambda i, e: (layer, 0, e)),
                  gate, gate, gate, gate],
        out_specs=pl.BlockSpec((d, tl), lambda i, e: (0, i)),
        out_shape=jax.ShapeDtypeStruct((d, t), F32),
        scratch_shapes=[pltpu.VMEM((2, PEER_PIECE * PEER_NKEYS, tl), F32),
                        pltpu.VMEM((2, PEER_PIECE * PEER_NKEYS, tl), BF16)],
        compiler_params=_params("parallel", "arbitrary"),
        name="peer_dense",
    )(h_t, u_bf, vt_bf, a, cnt, b, rank)


def kernel(x, c, w_ada, b_ada, w_in, ssm_lam_re, ssm_lam_im, ssm_log_step, ssm_b_re, ssm_b_im, ssm_c_re, ssm_c_im, ssm_d, ssm_w_glu, dn_conv_w, dn_a_log, dn_dt_bias, dn_norm_w, w_out, ln1_g, ln1_b, peer_w_query, peer_sub_keys, peer_u, peer_v, ln2_g, ln2_b):
    bsz, seq, d = x.shape
    depth = w_ada.shape[0]
    t = bsz * seq
    alpha = (2.0 * depth) ** 0.25

    mod = _ada_mod(c, w_ada, b_ada)[:, :bsz]
    w_in_bf = w_in.astype(BF16)
    w_gate = jnp.zeros((depth, d, LANES), BF16).at[:, :, :2 * N_HEADS].set(w_in_bf[:, :, N_MAIN:])
    wq_t = peer_w_query.transpose(0, 2, 1).astype(BF16)
    u_bf = peer_u.astype(BF16)
    vt_bf = peer_v.transpose(0, 2, 1).astype(BF16)
    xf = x.reshape(t, d)
    for l in range(depth):
        sh1, sc1, gt1, sh2, sc2, gt2 = [m.reshape(bsz, 1, d) for m in jnp.split(mod[l], 6, axis=-1)]

        hmix = _ln_mod(xf, sh1, sc1, seq)
        proj = _mm(hmix, w_in_bf, l, F32, 512, 1024, "in_proj", n=N_MAIN)
        gates_raw = _mm(hmix, w_gate, l, F32, 512, LANES, "gate_proj")

        prep = _s5_prep(ssm_lam_re[l], ssm_lam_im[l], ssm_log_step[l], ssm_b_re[l], ssm_b_im[l],
                        ssm_c_re[l], ssm_c_im[l], ssm_d[l])
        y_ssm = _s5_mixer(proj, bsz, seq, prep, ssm_w_glu[l].astype(BF16))
        y_dn = _dn_mixer(proj, gates_raw, bsz, seq, dn_conv_w[l], dn_a_log[l], dn_dt_bias[l],
                         dn_norm_w[l])
        xf = _out_ln(y_ssm, y_dn, w_out[l].astype(BF16), xf, gt1, ln1_g[l], ln1_b[l], seq, alpha)

        hffn_t = _ln_mod(xf, sh2, sc2, seq, feature_major=True)
        y_t = _peer_ffn(hffn_t, l, wq_t, peer_sub_keys[l], u_bf, vt_bf)
        xf = _res_ln(xf, y_t, gt2, ln2_g[l], ln2_b[l], seq, alpha)
    return xf.reshape(bsz, seq, d)
```

```python
import functools
import math

import jax
import jax.numpy as jnp
from jax import lax
from jax.experimental import pallas as pl
from jax.experimental.pallas import tpu as pltpu

F32 = jnp.float32
BF16 = jnp.bfloat16
HI = lax.Precision.HIGHEST

D_MODEL = 2048
D_SSM = 1024
SSM_GROUP = 16
N_GROUPS = D_SSM // SSM_GROUP
SSM_STATE = 64
SSM_CHUNK = 16
N_HEADS = 8
HEAD_DIM = 128
D_DN = N_HEADS * HEAD_DIM
DN_CHUNK = 64
CONV_WIDTH = 4
N_MAIN = D_SSM + 4 * D_DN
PEER_HEADS = 8
PEER_NKEYS = 128
PEER_EXPERTS = PEER_NKEYS * PEER_NKEYS
PEER_TOPK = 16
PEER_SUB = 8
PEER_TE = PEER_SUB * PEER_NKEYS
PEER_PIECE = 2
LN_EPS = 1e-5
NORM_EPS = 1e-6

LANES = 128
VMEM_LIMIT = 56 * 1024 * 1024


def _params(*sem):
    return pltpu.CompilerParams(dimension_semantics=sem, vmem_limit_bytes=VMEM_LIMIT)


def _gelu(x):
    return 0.5 * x * (1.0 + jnp.tanh(math.sqrt(2.0 / math.pi) * (x + 0.044715 * (x * x * x))))


def _sigmoid(x):
    return 1.0 / (1.0 + jnp.exp(-x))


def _dot(a, b, precision=None):
    return jnp.dot(a, b, precision=precision, preferred_element_type=F32)


def _dot_nt(a, b, precision=None):
    return lax.dot_general(a, b, (((1,), (1,)), ((), ())), precision=precision,
                           preferred_element_type=F32)


def _dot_tn(a, b, precision=None):
    return lax.dot_general(a, b, (((0,), (0,)), ((), ())), precision=precision,
                           preferred_element_type=F32)


def _ada_kernel(c_ref, w_ref, b_ref, o_ref):
    c = c_ref[...]
    ca = c * _sigmoid(c)
    hi = ca.astype(BF16)
    lo = (ca - hi.astype(F32)).astype(BF16)
    w = w_ref[0].astype(BF16)
    o_ref[0] = _dot(hi, w) + _dot(lo, w) + b_ref[0]


def _ada_mod(c, w_ada, b_ada):
    depth, d, n = w_ada.shape
    tn = 1024
    cp = jnp.zeros((8, d), F32).at[: c.shape[0]].set(c)
    return pl.pallas_call(
        _ada_kernel,
        grid=(depth, n // tn),
        in_specs=[pl.BlockSpec((8, d), lambda l, j: (0, 0)),
                  pl.BlockSpec((1, d, tn), lambda l, j: (l, 0, j)),
                  pl.BlockSpec((1, 1, tn), lambda l, j: (l, 0, j))],
        out_specs=pl.BlockSpec((1, 8, tn), lambda l, j: (l, 0, j)),
        out_shape=jax.ShapeDtypeStruct((depth, 8, n), F32),
        compiler_params=_params("parallel", "parallel"),
        name="ada_mod",
    )(cp, w_ada, b_ada.reshape(depth, 1, n))


def _ln(x):
    mu = jnp.mean(x, axis=-1, keepdims=True)
    xc = x - mu
    var = jnp.mean(xc * xc, axis=-1, keepdims=True)
    return xc * lax.rsqrt(var + LN_EPS)


def _ln_mod_kernel(x_ref, sh_ref, sc_ref, o_ref, *, feature_major):
    y = _ln(x_ref[...]) * (1.0 + sc_ref[0]) + sh_ref[0]
    o_ref[...] = (y.T if feature_major else y).astype(o_ref.dtype)


def _ln_mod(x, sh, sc, seq, feature_major=False):
    t, d = x.shape
    tm = min(256, seq)
    per = seq // tm
    if feature_major:
        out_spec, out_shape = pl.BlockSpec((d, tm), lambda i: (0, i)), (d, t)
    else:
        out_spec, out_shape = pl.BlockSpec((tm, d), lambda i: (i, 0)), (t, d)
    return pl.pallas_call(
        functools.partial(_ln_mod_kernel, feature_major=feature_major),
        grid=(t // tm,),
        in_specs=[pl.BlockSpec((tm, d), lambda i: (i, 0)),
                  pl.BlockSpec((1, 1, d), lambda i: (i // per, 0, 0)),
                  pl.BlockSpec((1, 1, d), lambda i: (i // per, 0, 0))],
        out_specs=out_spec,
        out_shape=jax.ShapeDtypeStruct(out_shape, BF16),
        compiler_params=_params("parallel"),
        name="ln_mod",
    )(x, sh, sc)


def _res_ln_kernel(x_ref, yt_ref, gt_ref, g_ref, b_ref, o_ref, *, alpha):
    r = alpha * x_ref[...] + gt_ref[0] * yt_ref[...].T
    o_ref[...] = _ln(r) * g_ref[...] + b_ref[...]


def _res_ln(x, y_t, gt, gain, bias, seq, alpha):
    t, d = x.shape
    tm = min(256, seq)
    per = seq // tm
    return pl.pallas_call(
        functools.partial(_res_ln_kernel, alpha=alpha),
        grid=(t // tm,),
        in_specs=[pl.BlockSpec((tm, d), lambda i: (i, 0)),
                  pl.BlockSpec((d, tm), lambda i: (0, i)),
                  pl.BlockSpec((1, 1, d), lambda i: (i // per, 0, 0)),
                  pl.BlockSpec((1, d), lambda i: (0, 0)),
                  pl.BlockSpec((1, d), lambda i: (0, 0))],
        out_specs=pl.BlockSpec((tm, d), lambda i: (i, 0)),
        out_shape=jax.ShapeDtypeStruct((t, d), F32),
        compiler_params=_params("parallel"),
        name="res_ln",
    )(x, y_t, gt, gain.reshape(1, d), bias.reshape(1, d))


def _mm_kernel(a_ref, b_ref, o_ref):
    o_ref[...] = _dot(a_ref[...], b_ref[...]).astype(o_ref.dtype)


def _mm(a, b, layer, out_dtype, tm, tn, name, n=None):
    m, k = a.shape
    n = b.shape[2] if n is None else n
    tm, tn = min(tm, m), min(tn, n)
    return pl.pallas_call(
        _mm_kernel,
        grid=(m // tm, n // tn),
        in_specs=[pl.BlockSpec((tm, k), lambda i, j: (i, 0)),
                  pl.BlockSpec((None, k, tn), lambda i, j: (layer, 0, j))],
        out_specs=pl.BlockSpec((tm, tn), lambda i, j: (i, j)),
        out_shape=jax.ShapeDtypeStruct((m, n), out_dtype),
        compiler_params=_params("parallel", "parallel"),
        name=name,
    )(a, b)


def _s5_prep(lam_re, lam_im, log_step, b_re, b_im, c_re, c_im, d_skip):
    g, p, nch = N_GROUPS, SSM_STATE, SSM_CHUNK
    gpt = LANES // SSM_GROUP
    ntile = g // gpt
    step = jnp.exp(log_step)[:, None]
    zr, zi = lam_re * step, lam_im * step
    ks = jnp.arange(nch + 1, dtype=F32)[:, None, None]
    mag = jnp.exp(ks * zr)
    pr, pi = mag * jnp.cos(ks * zi), mag * jnp.sin(ks * zi)
    nr, ni = pr[1] - 1.0, pi[1]
    den = lam_re * lam_re + lam_im * lam_im
    fr = (nr * lam_re + ni * lam_im) / den
    fi = (ni * lam_re - nr * lam_im) / den
    bbr = fr[..., None] * b_re - fi[..., None] * b_im
    bbi = fr[..., None] * b_im + fi[..., None] * b_re
    er = pr[:nch, :, :, None] * bbr - pi[:nch, :, :, None] * bbi
    ei = pr[:nch, :, :, None] * bbi + pi[:nch, :, :, None] * bbr
    kk = (jnp.einsum('gap,kgpc->kgac', c_re, er, precision=HI)
          - jnp.einsum('gap,kgpc->kgac', c_im, ei, precision=HI))
    wr = c_re[None] * pr[1:, :, None, :] - c_im[None] * pi[1:, :, None, :]
    wi = c_re[None] * pi[1:, :, None, :] + c_im[None] * pr[1:, :, None, :]

    def compact(m):
        m = m.reshape(nch, ntile, LANES, p).astype(BF16)
        return jnp.concatenate([m, m], axis=-1)

    bs_re, bs_im = compact(er[::-1].swapaxes(2, 3)), compact(ei[::-1].swapaxes(2, 3))
    mo_re, mo_im = compact(wr), compact(-wi)
    eye = jnp.eye(gpt, dtype=F32)[:, None, :, None]
    bk = kk.swapaxes(2, 3).reshape(nch, ntile, gpt, SSM_GROUP, 1, SSM_GROUP) * eye
    bk = bk.reshape(nch // 2, 2, ntile, LANES, LANES).astype(BF16)
    a_re, a_im = pr[nch].reshape(1, g * p), pi[nch].reshape(1, g * p)
    return bs_re, bs_im, bk, mo_re, mo_im, a_re, a_im, d_skip.reshape(1, D_SSM)


def _blockdiag_states(m, keep):
    reps = keep.shape[1] // m.shape[1]
    return jnp.where(keep, jnp.concatenate([m] * reps, axis=1), jnp.zeros((), m.dtype))


def _state_mask(width):
    rows = lax.broadcasted_iota(jnp.int32, (LANES, width), 0) // SSM_GROUP
    cols = lax.broadcasted_iota(jnp.int32, (LANES, width), 1) // SSM_STATE
    return rows == cols


def _s5_state_kernel(u_ref, bsr_ref, bsi_ref, sr_ref, si_ref):
    ncs = sr_ref.shape[0]
    keep = _state_mask(sr_ref.shape[1])
    acc_r = jnp.zeros(sr_ref.shape, F32)
    acc_i = jnp.zeros(si_ref.shape, F32)
    for s in range(SSM_CHUNK):
        xs = u_ref[0, pl.ds(s, ncs, stride=SSM_CHUNK), :].astype(BF16)
        acc_r = acc_r + _dot(xs, _blockdiag_states(bsr_ref[s], keep))
        acc_i = acc_i + _dot(xs, _blockdiag_states(bsi_ref[s], keep))
    sr_ref[...] = acc_r
    si_ref[...] = acc_i


def _s5_scan_kernel(sr_ref, si_ref, ar_ref, ai_ref, hr_ref, hi_ref):
    ar, ai = ar_ref[...], ai_ref[...]
    sub = 8

    def group(i, carry):
        hr, hi = carry
        rows = pl.ds(pl.multiple_of(i * sub, sub), sub)
        sr, si = sr_ref[rows, :], si_ref[rows, :]
        out_r, out_i = [], []
        for r in range(sub):
            out_r.append(hr)
            out_i.append(hi)
            hr, hi = ar * hr - ai * hi + sr[r:r + 1, :], ar * hi + ai * hr + si[r:r + 1, :]
        hr_ref[rows, :] = jnp.concatenate(out_r, axis=0)
        hi_ref[rows, :] = jnp.concatenate(out_i, axis=0)
        return hr, hi

    zero = jnp.zeros(ar.shape, F32)
    lax.fori_loop(0, sr_ref.shape[0] // sub, group, (zero, zero))


def _s5_out_kernel(u_ref, hr_ref, hi_ref, bk_ref, mor_ref, moi_ref, d_ref, o_ref, acc_s):
    x = u_ref[0]
    ncs = hr_ref.shape[0]
    pos = lax.broadcasted_iota(jnp.int32, x.shape, 0) % SSM_CHUNK

    def lagged(k):
        if k == 0:
            return x.astype(BF16)
        return jnp.where(pos >= k, pltpu.roll(x, k, axis=0), 0.0).astype(BF16)

    acc = d_ref[...] * x
    for kp in range(SSM_CHUNK // 2):
        xx = jnp.concatenate([lagged(2 * kp), lagged(2 * kp + 1)], axis=1)
        acc = acc + _dot(xx, bk_ref[kp].reshape(2 * LANES, LANES))
    acc_s[...] = acc
    hr, hi = hr_ref[...].astype(BF16), hi_ref[...].astype(BF16)
    keep = _state_mask(hr.shape[1])
    for r in range(SSM_CHUNK):
        rows = pl.ds(r, ncs, stride=SSM_CHUNK)
        acc_s[rows, :] = (acc_s[rows, :] + _dot_nt(hr, _blockdiag_states(mor_ref[r], keep))
                          + _dot_nt(hi, _blockdiag_states(moi_ref[r], keep)))
    o_ref[0] = _gelu(acc_s[...]).astype(o_ref.dtype)


def _glu_kernel(y_ref, wa_ref, wb_ref, o_ref):
    y = y_ref[...]
    o_ref[...] = (_dot(y, wa_ref[...]) * _sigmoid(_dot(y, wb_ref[...]))).astype(o_ref.dtype)


def _s5_mixer(proj, bsz, seq, prep, w_glu):
    bs_re, bs_im, bk, mo_re, mo_im, a_re, a_im, dvec = prep
    t = bsz * seq
    ncs = seq // SSM_CHUNK
    ntile = D_SSM // LANES
    sw = LANES // SSM_GROUP * SSM_STATE
    nstate = N_GROUPS * SSM_STATE
    proj3 = proj.reshape(bsz, seq, N_MAIN)
    u_spec = pl.BlockSpec((1, seq, LANES), lambda b, j: (b, 0, j))
    st_spec = pl.BlockSpec((ncs, sw), lambda b, j: (b, j))
    par = lambda m: pl.BlockSpec((m.shape[0], None, *m.shape[2:]), lambda b, j: (0, j, 0, 0))
    st_shape = jax.ShapeDtypeStruct((bsz * ncs, nstate), F32)

    s_re, s_im = pl.pallas_call(
        _s5_state_kernel,
        grid=(bsz, ntile),
        in_specs=[u_spec, par(bs_re), par(bs_im)],
        out_specs=[st_spec, st_spec],
        out_shape=[st_shape, st_shape],
        compiler_params=_params("parallel", "parallel"),
        name="s5_state",
    )(proj3, bs_re, bs_im)

    lb = 1024
    scan_spec = pl.BlockSpec((ncs, lb), lambda b, i: (b, i))
    coef_spec = pl.BlockSpec((1, lb), lambda b, i: (0, i))
    h_re, h_im = pl.pallas_call(
        _s5_scan_kernel,
        grid=(bsz, nstate // lb),
        in_specs=[scan_spec, scan_spec, coef_spec, coef_spec],
        out_specs=[scan_spec, scan_spec],
        out_shape=[st_shape, st_shape],
        compiler_params=_params("parallel", "parallel"),
        name="s5_scan",
    )(s_re, s_im, a_re, a_im)

    y = pl.pallas_call(
        _s5_out_kernel,
        grid=(bsz, ntile),
        in_specs=[u_spec, st_spec, st_spec,
                  pl.BlockSpec((*bk.shape[:2], None, LANES, LANES), lambda b, j: (0, 0, j, 0, 0)),
                  par(mo_re), par(mo_im), pl.BlockSpec((1, LANES), lambda b, j: (0, j))],
        out_specs=u_spec,
        out_shape=jax.ShapeDtypeStruct((bsz, seq, D_SSM), BF16),
        scratch_shapes=[pltpu.VMEM((seq, LANES), F32)],
        compiler_params=_params("parallel", "parallel"),
        name="s5_out",
    )(proj3, h_re, h_im, bk, mo_re, mo_im, dvec).reshape(t, D_SSM)

    tm, tn = min(512, t), 512
    nj = D_SSM // tn
    return pl.pallas_call(
        _glu_kernel,
        grid=(t // tm, nj),
        in_specs=[pl.BlockSpec((tm, D_SSM), lambda i, j: (i, 0)),
                  pl.BlockSpec((D_SSM, tn), lambda i, j: (0, j)),
                  pl.BlockSpec((D_SSM, tn), lambda i, j: (0, j + nj))],
        out_specs=pl.BlockSpec((tm, tn), lambda i, j: (i, j)),
        out_shape=jax.ShapeDtypeStruct((t, D_SSM), BF16),
        compiler_params=_params("parallel", "parallel"),
        name="s5_glu",
    )(y, w_glu, w_glu)


def _dn_conv_kernel(x_ref, w_ref, o_ref):
    j = pl.program_id(1)
    x = x_ref[0]
    w = w_ref[...]
    row = lax.broadcasted_iota(jnp.int32, x.shape, 0)
    acc = x * w[CONV_WIDTH - 1:CONV_WIDTH, :]
    for k in range(CONV_WIDTH - 1):
        sh = CONV_WIDTH - 1 - k
        xs = jnp.where(row >= sh, pltpu.roll(x, sh, axis=0), 0.0)
        acc = acc + xs * w[k:k + 1, :]
    y = acc * _sigmoid(acc)
    nrm = y * lax.rsqrt(jnp.sum(y * y, axis=-1, keepdims=True) + NORM_EPS)
    qscale = jnp.where(j < N_HEADS, HEAD_DIM ** -0.5, 1.0).astype(F32)
    o_ref[0] = jnp.where(j < 2 * N_HEADS, nrm * qscale, y)


def _gates_kernel(x_ref, alog_ref, dtb_ref, o_ref):
    x = x_ref[...]
    lane = lax.broadcasted_iota(jnp.int32, x.shape, 1)
    xs = x + dtb_ref[...]
    softplus = jnp.maximum(xs, 0.0) + jnp.log1p(jnp.exp(-jnp.abs(xs)))
    o_ref[...] = jnp.where(lane < N_HEADS, _sigmoid(x), -jnp.exp(alog_ref[...]) * softplus)


def _dn_kernel(q_ref, k_ref, v_ref, z_ref, gt_ref, nw_ref, o_ref, st_s, *, nchunks):
    c, dh = DN_CHUNK, HEAD_DIM

    @pl.when(pl.program_id(1) == 0)
    def _():
        st_s[...] = jnp.zeros_like(st_s)

    ri = lax.broadcasted_iota(jnp.int32, (c, c), 0)
    ci = lax.broadcasted_iota(jnp.int32, (c, c), 1)
    tril, strict = ri >= ci, ri > ci
    tril_f = tril.astype(F32)
    triu_f = (ri <= ci).astype(F32)
    eye = (ri == ci).astype(F32)
    bf = lambda x: x.astype(BF16)

    def chunk(n, carry):
        rows = pl.ds(pl.multiple_of(n * c, c), c)
        gt = gt_ref[0, rows, :]
        gcum_c = _dot(tril_f, gt, HI)
        gcum_r = _dot_tn(gt, triu_f, HI)
        heads = range(N_HEADS)
        cols = [slice(h * dh, (h + 1) * dh) for h in heads]
        beta = [gt[:, h:h + 1] for h in heads]
        gc = [gcum_c[:, N_HEADS + h:N_HEADS + h + 1] for h in heads]
        gr = [gcum_r[N_HEADS + h:N_HEADS + h + 1, :] for h in heads]
        gc_b = [jnp.broadcast_to(gc[h], (c, dh)) for h in heads]
        g_last = [gc_b[h][c - 1:c, :] for h in heads]
        decay = [jnp.where(tril, jnp.exp(jnp.where(tril, gc[h] - gr[h], 0.0)), 0.0) for h in heads]
        q = [q_ref[0, rows, cols[h]] for h in heads]
        k = [k_ref[0, rows, cols[h]] for h in heads]
        kb = [k[h] * beta[h] for h in heads]
        k16 = [bf(k[h]) for h in heads]
        lmat = [jnp.where(strict, _dot_nt(bf(kb[h]), k16[h]) * decay[h], 0.0) for h in heads]
        attn = [bf(jnp.where(tril, _dot_nt(bf(q[h]), k16[h]) * decay[h], 0.0)) for h in heads]
        tmat = [eye - lmat[h] for h in heads]
        lp16 = [bf(lmat[h]) for h in heads]
        lp16 = [bf(_dot(lp16[h], lp16[h])) for h in heads]
        for step in range(5):
            tmat = [tmat[h] + _dot(bf(tmat[h]), lp16[h]) for h in heads]
            if step < 4:
                lp16 = [bf(_dot(lp16[h], lp16[h])) for h in heads]
        t16 = [bf(tmat[h]) for h in heads]
        egc = [jnp.exp(gc_b[h]) for h in heads]
        w_val = [_dot(t16[h], bf(v_ref[0, rows, cols[h]] * beta[h])) for h in heads]
        k_cum = [bf(_dot(t16[h], bf(kb[h] * egc[h]))) for h in heads]
        state = [st_s[h] for h in heads]
        s16 = [bf(state[h]) for h in heads]
        out = [_dot(bf(q[h] * egc[h]), s16[h]) for h in heads]
        vn16 = [bf(w_val[h] - _dot(k_cum[h], s16[h])) for h in heads]
        out = [out[h] + _dot(attn[h], vn16[h]) for h in heads]
        for h in heads:
            k_tail = bf(k[h] * jnp.exp(g_last[h] - gc_b[h]))
            st_s[h] = state[h] * jnp.exp(g_last[h]) + _dot_tn(k_tail, vn16[h])
        for h in heads:
            o = out[h] * lax.rsqrt(jnp.mean(out[h] * out[h], axis=-1, keepdims=True) + NORM_EPS)
            z = z_ref[0, rows, cols[h]]
            o_ref[0, rows, cols[h]] = (o * nw_ref[...] * (z * _sigmoid(z))).astype(o_ref.dtype)
        return carry

    lax.fori_loop(0, nchunks, chunk, 0)


def _dn_mixer(proj, gates_raw, bsz, seq, conv_w, a_log, dt_bias, norm_w):
    t = bsz * seq
    proj3 = proj.reshape(bsz, seq, N_MAIN)
    qkv_off = D_SSM // LANES
    n_tiles = 3 * D_DN // LANES
    qkv = pl.pallas_call(
        _dn_conv_kernel,
        grid=(bsz, n_tiles),
        in_specs=[pl.BlockSpec((1, seq, LANES), lambda b, j: (b, 0, j + qkv_off)),
                  pl.BlockSpec((CONV_WIDTH, LANES), lambda b, j: (0, j))],
        out_specs=pl.BlockSpec((1, seq, LANES), lambda b, j: (b, 0, j)),
        out_shape=jax.ShapeDtypeStruct((bsz, seq, 3 * D_DN), F32),
        compiler_params=_params("parallel", "parallel"),
        name="dn_conv",
    )(proj3, conv_w)

    pad = jnp.zeros((1, LANES), F32)
    alog_p = pad.at[0, N_HEADS:2 * N_HEADS].set(a_log)
    dtb_p = pad.at[0, N_HEADS:2 * N_HEADS].set(dt_bias)
    tm = min(1024, t)
    gates = pl.pallas_call(
        _gates_kernel,
        grid=(t // tm,),
        in_specs=[pl.BlockSpec((tm, LANES), lambda i: (i, 0)),
                  pl.BlockSpec((1, LANES), lambda i: (0, 0)),
                  pl.BlockSpec((1, LANES), lambda i: (0, 0))],
        out_specs=pl.BlockSpec((tm, LANES), lambda i: (i, 0)),
        out_shape=jax.ShapeDtypeStruct((t, LANES), F32),
        compiler_params=_params("parallel"),
        name="dn_gates",
    )(gates_raw, alog_p, dtb_p).reshape(bsz, seq, LANES)

    tb = min(512, seq)
    part = lambda off: pl.BlockSpec((1, tb, D_DN), lambda b, i: (b, i, off))
    return pl.pallas_call(
        functools.partial(_dn_kernel, nchunks=tb // DN_CHUNK),
        grid=(bsz, seq // tb),
        in_specs=[part(0), part(1), part(2), part((D_SSM + 3 * D_DN) // D_DN),
                  pl.BlockSpec((1, tb, LANES), lambda b, i: (b, i, 0)),
                  pl.BlockSpec((1, HEAD_DIM), lambda b, i: (0, 0))],
        out_specs=pl.BlockSpec((1, tb, D_DN), lambda b, i: (b, i, 0)),
        out_shape=jax.ShapeDtypeStruct((bsz, seq, D_DN), BF16),
        scratch_shapes=[pltpu.VMEM((N_HEADS, HEAD_DIM, HEAD_DIM), F32)],
        compiler_params=_params("parallel", "arbitrary"),
        name="dn_chunk",
    )(qkv, qkv, qkv, proj3, gates, norm_w.reshape(1, HEAD_DIM)).reshape(t, D_DN)


def _out_ln_kernel(a1_ref, a2_ref, w1_ref, w2_ref, x_ref, gt_ref, g_ref, b_ref, o_ref, *, alpha):
    y = _dot(a1_ref[...], w1_ref[...]) + _dot(a2_ref[...], w2_ref[...])
    r = alpha * x_ref[...] + gt_ref[0] * y
    o_ref[...] = _ln(r) * g_ref[...] + b_ref[...]


def _out_ln(y_ssm, y_dn, w_out, x, gt, gain, bias, seq, alpha):
    t, d = x.shape
    tm = min(256, seq)
    per = seq // tm
    return pl.pallas_call(
        functools.partial(_out_ln_kernel, alpha=alpha),
        grid=(t // tm,),
        in_specs=[pl.BlockSpec((tm, D_SSM), lambda i: (i, 0)),
                  pl.BlockSpec((tm, D_DN), lambda i: (i, 0)),
                  pl.BlockSpec((D_SSM, d), lambda i: (0, 0)),
                  pl.BlockSpec((D_DN, d), lambda i: (1, 0)),
                  pl.BlockSpec((tm, d), lambda i: (i, 0)),
                  pl.BlockSpec((1, 1, d), lambda i: (i // per, 0, 0)),
                  pl.BlockSpec((1, d), lambda i: (0, 0)),
                  pl.BlockSpec((1, d), lambda i: (0, 0))],
        out_specs=pl.BlockSpec((tm, d), lambda i: (i, 0)),
        out_shape=jax.ShapeDtypeStruct((t, d), F32),
        compiler_params=_params("parallel"),
        name="out_ln",
    )(y_ssm, y_dn, w_out, w_out, x, gt, gain.reshape(1, d), bias.reshape(1, d))


def _pack_rows(x):
    return pltpu.bitcast(x.astype(BF16), jnp.uint32)


def _unpack_rows(x):
    return pltpu.bitcast(x, BF16)


def _top_values(s, count):
    vals = []
    rank = jnp.full(s.shape, float(count), F32)
    for r in range(count):
        m = jnp.max(s, axis=0, keepdims=True)
        vals.append(m)
        hit = s == m
        rank = jnp.where(hit, float(r), rank)
        s = jnp.where(hit, -jnp.inf, s)
    return jnp.concatenate(vals, axis=0), rank


def _peer_route_kernel(ht_ref, wq_ref, keys_ref, a_ref, cnt_ref, b_ref, rank_ref, q_s):
    q_s[...] = _dot(wq_ref[...], ht_ref[...])
    k = PEER_TOPK

    def head(h, carry):
        base = pl.multiple_of(h * 2 * PEER_NKEYS, 2 * PEER_NKEYS)
        s0 = _dot(keys_ref[h, 0], q_s[pl.ds(base, PEER_NKEYS), :], HI)
        s1 = _dot(keys_ref[h, 1], q_s[pl.ds(base + PEER_NKEYS, PEER_NKEYS), :], HI)
        v0, rank0 = _top_values(s0, k)
        v1, rank1 = _top_values(s1, k)
        half = k // 2
        sub = lax.broadcasted_iota(jnp.int32, (half, v0.shape[1]), 0)
        cands = [v0[0:1] + v1[:half], v0[0:1] + v1[half:]]
        for x in range(1, half):
            cands.append(jnp.where(sub < k // (x + 1), v0[x:x + 1] + v1[:half], -jnp.inf))
        cands.append(v0[half:] + v1[0:1])
        top = v0[0:1] + v1[0:1]
        z = jnp.zeros_like(top)
        thr = top
        for r in range(k):
            m = cands[0]
            for cnd in cands[1:]:
                m = jnp.maximum(m, cnd)
            m = jnp.max(m, axis=0, keepdims=True)
            z = z + jnp.exp(m - top)
            thr = m
            if r + 1 < k:
                cands = [jnp.where(cnd == m, -jnp.inf, cnd) for cnd in cands]
        cnt_top = jnp.zeros_like(v0)
        for r in range(k):
            cnt_top = cnt_top + jnp.where(v0 + v1[r:r + 1] >= thr, 1.0, 0.0)
        cnt = jnp.zeros_like(s0)
        for x in range(k):
            cnt = jnp.where(rank0 == float(x), cnt_top[x:x + 1], cnt)
        a_ref[h] = jnp.exp(s0 - v0[0:1]) / z
        b_ref[h] = _pack_rows(jnp.exp(s1 - v1[0:1]))
        cnt_ref[h] = cnt
        rank_ref[h] = _pack_rows(rank1)
        return carry

    lax.fori_loop(0, PEER_HEADS, head, 0)


def _peer_dense_kernel(ht_ref, u_ref, vt_ref, a_ref, cnt_ref, b_ref, rank_ref, o_ref,
                       act_s, p_s, *, tl):
    e = pl.program_id(1)

    @pl.when(e == 0)
    def _():
        o_ref[...] = jnp.zeros_like(o_ref)

    first = pl.ds(pl.multiple_of(e * PEER_SUB, PEER_SUB), PEER_SUB)
    grp = PEER_PIECE * PEER_NKEYS
    npieces = PEER_TE // grp

    def up(pc):
        act_s[pc % 2] = _dot(u_ref[pc * grp:(pc + 1) * grp, :], ht_ref[...])

    def down(pc):
        o_ref[...] += _dot(vt_ref[:, pc * grp:(pc + 1) * grp], p_s[pc % 2])

    def gate(pc):
        for lc in range(tl // LANES):
            ls = slice(lc * LANES, (lc + 1) * LANES)
            for j in range(PEER_PIECE):
                ii = PEER_PIECE * pc + j
                er = slice(j * PEER_NKEYS, (j + 1) * PEER_NKEYS)
                w = jnp.zeros((PEER_NKEYS, LANES), BF16)
                for h in range(PEER_HEADS):
                    row = lambda ref: jnp.broadcast_to(ref[h, first, ls][ii:ii + 1, :],
                                                       (PEER_NKEYS, LANES)).astype(BF16)
                    keep = _unpack_rows(rank_ref[h, :, ls]) < row(cnt_ref)
                    w = w + jnp.where(keep, _unpack_rows(b_ref[h, :, ls]), jnp.zeros((), BF16)) * row(a_ref)
                g = _gelu(act_s[pc % 2, er, ls]).astype(BF16)
                p_s[pc % 2, er, ls] = g * w

    up(0)
    for pc in range(npieces):
        if pc + 1 < npieces:
            up(pc + 1)
        if pc >= 1:
            down(pc - 1)
        gate(pc)
    down(npieces - 1)


def _peer_ffn(h_t, layer, wq_t, keys, u_bf, vt_bf):
    d, t = h_t.shape
    nq = wq_t.shape[1]
    tl = min(256, t)
    gate = pl.BlockSpec((PEER_HEADS, PEER_NKEYS, tl), lambda i: (0, 0, i))
    pair = pl.BlockSpec((PEER_HEADS, PEER_NKEYS // 2, tl), lambda i: (0, 0, i))
    gshape = jax.ShapeDtypeStruct((PEER_HEADS, PEER_NKEYS, t), F32)
    pshape = jax.ShapeDtypeStruct((PEER_HEADS, PEER_NKEYS // 2, t), jnp.uint32)
    a, cnt, b, rank = pl.pallas_call(
        _peer_route_kernel,
        grid=(t // tl,),
        in_specs=[pl.BlockSpec((d, tl), lambda i: (0, i)),
                  pl.BlockSpec((None, nq, d), lambda i: (layer, 0, 0)),
                  pl.BlockSpec((PEER_HEADS, 2, PEER_NKEYS, PEER_NKEYS), lambda i: (0, 0, 0, 0))],
        out_specs=[gate, gate, pair, pair],
        out_shape=[gshape, gshape, pshape, pshape],
        scratch_shapes=[pltpu.VMEM((nq, tl), F32)],
        compiler_params=_params("parallel"),
        name="peer_route",
    )(h_t, wq_t, keys)

    tl, te = min(1024, t), PEER_TE
    once = pl.Buffered(1)
    gate = pl.BlockSpec((PEER_HEADS, PEER_NKEYS, tl), lambda i, e: (0, 0, i), pipeline_mode=once)
    pair = pl.BlockSpec((PEER_HEADS, PEER_NKEYS // 2, tl), lambda i, e: (0, 0, i), pipeline_mode=once)
    return pl.pallas_call(
        functools.partial(_peer_dense_kernel, tl=tl),
        grid=(t // tl, PEER_EXPERTS // te),
        in_specs=[pl.BlockSpec((d, tl), lambda i, e: (0, i), pipeline_mode=once),
                  pl.BlockSpec((None, te, d), lambda i, e: (layer, e, 0)),
                  pl.BlockSpec((None, None, d, te), lambda i, e: (layer, e, 0, 0)),
                  gate, gate, pair, pair],
        out_specs=pl.BlockSpec((d, tl), lambda i, e: (0, i), pipeline_mode=once),
        out_shape=jax.ShapeDtypeStruct((d, t), F32),
        scratch_shapes=[pltpu.VMEM((2, PEER_PIECE * PEER_NKEYS, tl), F32),
                        pltpu.VMEM((2, PEER_PIECE * PEER_NKEYS, tl), BF16)],
        compiler_params=_params("parallel", "arbitrary"),
        name="peer_dense",
    )(h_t, u_bf, vt_bf, a, cnt, b, rank)


def kernel(x, c, w_ada, b_ada, w_in, ssm_lam_re, ssm_lam_im, ssm_log_step, ssm_b_re, ssm_b_im, ssm_c_re, ssm_c_im, ssm_d, ssm_w_glu, dn_conv_w, dn_a_log, dn_dt_bias, dn_norm_w, w_out, ln1_g, ln1_b, peer_w_query, peer_sub_keys, peer_u, peer_v, ln2_g, ln2_b):
    bsz, seq, d = x.shape
    depth = w_ada.shape[0]
    t = bsz * seq
    alpha = (2.0 * depth) ** 0.25

    mod = _ada_mod(c, w_ada, b_ada)[:, :bsz]
    w_in_bf = w_in.astype(BF16)
    w_gate = jnp.zeros((depth, d, LANES), BF16).at[:, :, :2 * N_HEADS].set(w_in_bf[:, :, N_MAIN:])
    wq_t = peer_w_query.transpose(0, 2, 1).astype(BF16)
    u_bf = peer_u.astype(BF16)
    vt_bf = peer_v.reshape(depth, -1, PEER_TE, d).transpose(0, 1, 3, 2).astype(BF16)
    xf = x.reshape(t, d)
    for l in range(depth):
        sh1, sc1, gt1, sh2, sc2, gt2 = [m.reshape(bsz, 1, d) for m in jnp.split(mod[l], 6, axis=-1)]

        hmix = _ln_mod(xf, sh1, sc1, seq)
        proj = _mm(hmix, w_in_bf, l, F32, 512, 1024, "in_proj", n=N_MAIN)
        gates_raw = _mm(hmix, w_gate, l, F32, 512, LANES, "gate_proj")

        prep = _s5_prep(ssm_lam_re[l], ssm_lam_im[l], ssm_log_step[l], ssm_b_re[l], ssm_b_im[l],
                        ssm_c_re[l], ssm_c_im[l], ssm_d[l])
        y_ssm = _s5_mixer(proj, bsz, seq, prep, ssm_w_glu[l].astype(BF16))
        y_dn = _dn_mixer(proj, gates_raw, bsz, seq, dn_conv_w[l], dn_a_log[l], dn_dt_bias[l],
                         dn_norm_w[l])
        xf = _out_ln(y_ssm, y_dn, w_out[l].astype(BF16), xf, gt1, ln1_g[l], ln1_b[l], seq, alpha)

        hffn_t = _ln_mod(xf, sh2, sc2, seq, feature_major=True)
        y_t = _peer_ffn(hffn_t, l, wq_t, peer_sub_keys[l], u_bf, vt_bf)
        xf = _res_ln(xf, y_t, gt2, ln2_g[l], ln2_b[l], seq, alpha)
    return xf.reshape(bsz, seq, d)
```

```python
import functools
import math

import jax
import jax.numpy as jnp
from jax import lax
from jax.experimental import pallas as pl
from jax.experimental.pallas import tpu as pltpu

F32 = jnp.float32
BF16 = jnp.bfloat16
HI = lax.Precision.HIGHEST

D_MODEL = 2048
D_SSM = 1024
SSM_GROUP = 16
N_GROUPS = D_SSM // SSM_GROUP
SSM_STATE = 64
SSM_CHUNK = 16
N_HEADS = 8
HEAD_DIM = 128
D_DN = N_HEADS * HEAD_DIM
DN_CHUNK = 64
CONV_WIDTH = 4
DN_HIST = 8
N_MAIN = D_SSM + 4 * D_DN
PEER_HEADS = 8
PEER_NKEYS = 128
PEER_EXPERTS = PEER_NKEYS * PEER_NKEYS
PEER_TOPK = 16
PEER_SUB = 8
PEER_TE = PEER_SUB * PEER_NKEYS
PEER_PIECE = 2
LN_EPS = 1e-5
NORM_EPS = 1e-6

LANES = 128
VMEM_LIMIT = 56 * 1024 * 1024


def _params(*sem):
    return pltpu.CompilerParams(dimension_semantics=sem, vmem_limit_bytes=VMEM_LIMIT)


def _gelu(x):
    return 0.5 * x * (1.0 + jnp.tanh(math.sqrt(2.0 / math.pi) * (x + 0.044715 * (x * x * x))))


def _sigmoid(x):
    return 1.0 / (1.0 + jnp.exp(-x))


def _dot(a, b, precision=None):
    return jnp.dot(a, b, precision=precision, preferred_element_type=F32)


def _dot_nt(a, b, precision=None):
    return lax.dot_general(a, b, (((1,), (1,)), ((), ())), precision=precision,
                           preferred_element_type=F32)


def _dot_tn(a, b, precision=None):
    return lax.dot_general(a, b, (((0,), (0,)), ((), ())), precision=precision,
                           preferred_element_type=F32)


def _ada_kernel(c_ref, w_ref, b_ref, o_ref):
    c = c_ref[...]
    ca = c * _sigmoid(c)
    hi = ca.astype(BF16)
    lo = (ca - hi.astype(F32)).astype(BF16)
    w = w_ref[0].astype(BF16)
    o_ref[0] = _dot(hi, w) + _dot(lo, w) + b_ref[0]


def _ada_mod(c, w_ada, b_ada):
    depth, d, n = w_ada.shape
    tn = 1024
    cp = jnp.zeros((8, d), F32).at[: c.shape[0]].set(c)
    return pl.pallas_call(
        _ada_kernel,
        grid=(depth, n // tn),
        in_specs=[pl.BlockSpec((8, d), lambda l, j: (0, 0)),
                  pl.BlockSpec((1, d, tn), lambda l, j: (l, 0, j)),
                  pl.BlockSpec((1, 1, tn), lambda l, j: (l, 0, j))],
        out_specs=pl.BlockSpec((1, 8, tn), lambda l, j: (l, 0, j)),
        out_shape=jax.ShapeDtypeStruct((depth, 8, n), F32),
        compiler_params=_params("parallel", "parallel"),
        name="ada_mod",
    )(cp, w_ada, b_ada.reshape(depth, 1, n))


def _ln(x):
    mu = jnp.mean(x, axis=-1, keepdims=True)
    xc = x - mu
    var = jnp.mean(xc * xc, axis=-1, keepdims=True)
    return xc * lax.rsqrt(var + LN_EPS)


def _ln_mod_kernel(x_ref, sh_ref, sc_ref, o_ref):
    y = _ln(x_ref[...]) * (1.0 + sc_ref[0]) + sh_ref[0]
    o_ref[...] = y.T.astype(o_ref.dtype)


def _ln_mod_t(x, sh, sc, seq):
    t, d = x.shape
    tm = min(256, seq)
    per = seq // tm
    return pl.pallas_call(
        _ln_mod_kernel,
        grid=(t // tm,),
        in_specs=[pl.BlockSpec((tm, d), lambda i: (i, 0)),
                  pl.BlockSpec((1, 1, d), lambda i: (i // per, 0, 0)),
                  pl.BlockSpec((1, 1, d), lambda i: (i // per, 0, 0))],
        out_specs=pl.BlockSpec((d, tm), lambda i: (0, i)),
        out_shape=jax.ShapeDtypeStruct((d, t), BF16),
        compiler_params=_params("parallel"),
        name="ln_mod",
    )(x, sh, sc)


def _res_ln_kernel(x_ref, yt_ref, gt_ref, g_ref, b_ref, o_ref, *, alpha):
    r = alpha * x_ref[...] + gt_ref[0] * yt_ref[...].T
    o_ref[...] = _ln(r) * g_ref[...] + b_ref[...]


def _res_ln(x, y_t, gt, gain, bias, seq, alpha):
    t, d = x.shape
    tm = min(256, seq)
    per = seq // tm
    return pl.pallas_call(
        functools.partial(_res_ln_kernel, alpha=alpha),
        grid=(t // tm,),
        in_specs=[pl.BlockSpec((tm, d), lambda i: (i, 0)),
                  pl.BlockSpec((d, tm), lambda i: (0, i)),
                  pl.BlockSpec((1, 1, d), lambda i: (i // per, 0, 0)),
                  pl.BlockSpec((1, d), lambda i: (0, 0)),
                  pl.BlockSpec((1, d), lambda i: (0, 0))],
        out_specs=pl.BlockSpec((tm, d), lambda i: (i, 0)),
        out_shape=jax.ShapeDtypeStruct((t, d), F32),
        compiler_params=_params("parallel"),
        name="res_ln",
    )(x, y_t, gt, gain.reshape(1, d), bias.reshape(1, d))


def _in_proj_kernel(x_ref, sh_ref, sc_ref, w_ref, wg_ref, o_ref, g_ref, hm_s):
    @pl.when(pl.program_id(1) == 0)
    def _():
        hm = (_ln(x_ref[...]) * (1.0 + sc_ref[0]) + sh_ref[0]).astype(BF16)
        hm_s[...] = hm
        g_ref[...] = _dot(hm, wg_ref[...])

    o_ref[...] = _dot(hm_s[...], w_ref[...])


def _in_proj(x, sh, sc, w_in_bf, w_gate, layer, seq):
    t, d = x.shape
    tm, tn = min(512, seq), 1024
    per = seq // tm
    return pl.pallas_call(
        _in_proj_kernel,
        grid=(t // tm, N_MAIN // tn),
        in_specs=[pl.BlockSpec((tm, d), lambda i, j: (i, 0)),
                  pl.BlockSpec((1, 1, d), lambda i, j: (i // per, 0, 0)),
                  pl.BlockSpec((1, 1, d), lambda i, j: (i // per, 0, 0)),
                  pl.BlockSpec((None, d, tn), lambda i, j: (layer, 0, j)),
                  pl.BlockSpec((None, d, LANES), lambda i, j: (layer, 0, 0))],
        out_specs=[pl.BlockSpec((tm, tn), lambda i, j: (i, j)),
                   pl.BlockSpec((tm, LANES), lambda i, j: (i, 0))],
        out_shape=[jax.ShapeDtypeStruct((t, N_MAIN), F32), jax.ShapeDtypeStruct((t, LANES), F32)],
        scratch_shapes=[pltpu.VMEM((tm, d), BF16)],
        compiler_params=_params("parallel", "arbitrary"),
        name="in_proj",
    )(x, sh, sc, w_in_bf, w_gate)


def _s5_prep(lam_re, lam_im, log_step, b_re, b_im, c_re, c_im, d_skip):
    g, p, nch = N_GROUPS, SSM_STATE, SSM_CHUNK
    gpt = LANES // SSM_GROUP
    ntile = g // gpt
    step = jnp.exp(log_step)[:, None]
    zr, zi = lam_re * step, lam_im * step
    ks = jnp.arange(nch + 1, dtype=F32)[:, None, None]
    mag = jnp.exp(ks * zr)
    pr, pi = mag * jnp.cos(ks * zi), mag * jnp.sin(ks * zi)
    nr, ni = pr[1] - 1.0, pi[1]
    den = lam_re * lam_re + lam_im * lam_im
    fr = (nr * lam_re + ni * lam_im) / den
    fi = (ni * lam_re - nr * lam_im) / den
    bbr = fr[..., None] * b_re - fi[..., None] * b_im
    bbi = fr[..., None] * b_im + fi[..., None] * b_re
    er = pr[:nch, :, :, None] * bbr - pi[:nch, :, :, None] * bbi
    ei = pr[:nch, :, :, None] * bbi + pi[:nch, :, :, None] * bbr
    kk = (jnp.einsum('gap,kgpc->kgac', c_re, er, precision=HI)
          - jnp.einsum('gap,kgpc->kgac', c_im, ei, precision=HI))
    wr = c_re[None] * pr[1:, :, None, :] - c_im[None] * pi[1:, :, None, :]
    wi = c_re[None] * pi[1:, :, None, :] + c_im[None] * pr[1:, :, None, :]

    def compact(m):
        m = m.reshape(nch, ntile, LANES, p).astype(BF16)
        return jnp.concatenate([m, m], axis=-1)

    bs_re, bs_im = compact(er[::-1].swapaxes(2, 3)), compact(ei[::-1].swapaxes(2, 3))
    mo_re, mo_im = compact(wr), compact(-wi)
    eye = jnp.eye(gpt, dtype=F32)[:, None, :, None]
    bk = kk.swapaxes(2, 3).reshape(nch, ntile, gpt, SSM_GROUP, 1, SSM_GROUP) * eye
    bk = bk.reshape(nch // 2, 2, ntile, LANES, LANES).astype(BF16)
    a_re, a_im = pr[nch].reshape(1, g * p), pi[nch].reshape(1, g * p)
    return bs_re, bs_im, bk, mo_re, mo_im, a_re, a_im, d_skip.reshape(1, D_SSM)


def _blockdiag_states(m, keep):
    reps = keep.shape[1] // m.shape[1]
    return jnp.where(keep, jnp.concatenate([m] * reps, axis=1), jnp.zeros((), m.dtype))


def _state_mask(width):
    rows = lax.broadcasted_iota(jnp.int32, (LANES, width), 0) // SSM_GROUP
    cols = lax.broadcasted_iota(jnp.int32, (LANES, width), 1) // SSM_STATE
    return rows == cols


def _s5_state_kernel(u_ref, bsr_ref, bsi_ref, sr_ref, si_ref):
    ncs = sr_ref.shape[0]
    keep = _state_mask(sr_ref.shape[1])
    acc_r = jnp.zeros(sr_ref.shape, F32)
    acc_i = jnp.zeros(si_ref.shape, F32)
    for s in range(SSM_CHUNK):
        xs = u_ref[0, pl.ds(s, ncs, stride=SSM_CHUNK), :].astype(BF16)
        acc_r = acc_r + _dot(xs, _blockdiag_states(bsr_ref[s], keep))
        acc_i = acc_i + _dot(xs, _blockdiag_states(bsi_ref[s], keep))
    sr_ref[...] = acc_r
    si_ref[...] = acc_i


def _s5_scan_kernel(sr_ref, si_ref, ar_ref, ai_ref, hr_ref, hi_ref):
    ar, ai = ar_ref[...], ai_ref[...]
    sub = 8

    def group(i, carry):
        hr, hi = carry
        rows = pl.ds(pl.multiple_of(i * sub, sub), sub)
        sr, si = sr_ref[rows, :], si_ref[rows, :]
        out_r, out_i = [], []
        for r in range(sub):
            out_r.append(hr)
            out_i.append(hi)
            hr, hi = ar * hr - ai * hi + sr[r:r + 1, :], ar * hi + ai * hr + si[r:r + 1, :]
        hr_ref[rows, :] = jnp.concatenate(out_r, axis=0)
        hi_ref[rows, :] = jnp.concatenate(out_i, axis=0)
        return hr, hi

    zero = jnp.zeros(ar.shape, F32)
    lax.fori_loop(0, sr_ref.shape[0] // sub, group, (zero, zero))


def _s5_out_kernel(u_ref, hr_ref, hi_ref, bk_ref, mor_ref, moi_ref, d_ref, o_ref, acc_s):
    x = u_ref[0]
    ncs = hr_ref.shape[0]
    pos = lax.broadcasted_iota(jnp.int32, x.shape, 0) % SSM_CHUNK

    def lagged(k):
        if k == 0:
            return x.astype(BF16)
        return jnp.where(pos >= k, pltpu.roll(x, k, axis=0), 0.0).astype(BF16)

    acc = d_ref[...] * x
    for kp in range(SSM_CHUNK // 2):
        xx = jnp.concatenate([lagged(2 * kp), lagged(2 * kp + 1)], axis=1)
        acc = acc + _dot(xx, bk_ref[kp].reshape(2 * LANES, LANES))
    acc_s[...] = acc
    hr, hi = hr_ref[...].astype(BF16), hi_ref[...].astype(BF16)
    keep = _state_mask(hr.shape[1])
    for r in range(SSM_CHUNK):
        rows = pl.ds(r, ncs, stride=SSM_CHUNK)
        acc_s[rows, :] = (acc_s[rows, :] + _dot_nt(hr, _blockdiag_states(mor_ref[r], keep))
                          + _dot_nt(hi, _blockdiag_states(moi_ref[r], keep)))
    o_ref[0] = _gelu(acc_s[...]).astype(o_ref.dtype)


def _glu_kernel(y_ref, wa_ref, wb_ref, o_ref):
    y = y_ref[...]
    o_ref[...] = (_dot(y, wa_ref[...]) * _sigmoid(_dot(y, wb_ref[...]))).astype(o_ref.dtype)


def _s5_mixer(proj, bsz, seq, prep, w_glu):
    bs_re, bs_im, bk, mo_re, mo_im, a_re, a_im, dvec = prep
    t = bsz * seq
    ncs = seq // SSM_CHUNK
    ntile = D_SSM // LANES
    sw = LANES // SSM_GROUP * SSM_STATE
    nstate = N_GROUPS * SSM_STATE
    proj3 = proj.reshape(bsz, seq, N_MAIN)
    u_spec = pl.BlockSpec((1, seq, LANES), lambda b, j: (b, 0, j))
    st_spec = pl.BlockSpec((ncs, sw), lambda b, j: (b, j))
    par = lambda m: pl.BlockSpec((m.shape[0], None, *m.shape[2:]), lambda b, j: (0, j, 0, 0))
    st_shape = jax.ShapeDtypeStruct((bsz * ncs, nstate), F32)

    s_re, s_im = pl.pallas_call(
        _s5_state_kernel,
        grid=(bsz, ntile),
        in_specs=[u_spec, par(bs_re), par(bs_im)],
        out_specs=[st_spec, st_spec],
        out_shape=[st_shape, st_shape],
        compiler_params=_params("parallel", "parallel"),
        name="s5_state",
    )(proj3, bs_re, bs_im)

    lb = 1024
    scan_spec = pl.BlockSpec((ncs, lb), lambda b, i: (b, i))
    coef_spec = pl.BlockSpec((1, lb), lambda b, i: (0, i))
    h_re, h_im = pl.pallas_call(
        _s5_scan_kernel,
        grid=(bsz, nstate // lb),
        in_specs=[scan_spec, scan_spec, coef_spec, coef_spec],
        out_specs=[scan_spec, scan_spec],
        out_shape=[st_shape, st_shape],
        compiler_params=_params("parallel", "parallel"),
        name="s5_scan",
    )(s_re, s_im, a_re, a_im)

    y = pl.pallas_call(
        _s5_out_kernel,
        grid=(bsz, ntile),
        in_specs=[u_spec, st_spec, st_spec,
                  pl.BlockSpec((*bk.shape[:2], None, LANES, LANES), lambda b, j: (0, 0, j, 0, 0)),
                  par(mo_re), par(mo_im), pl.BlockSpec((1, LANES), lambda b, j: (0, j))],
        out_specs=u_spec,
        out_shape=jax.ShapeDtypeStruct((bsz, seq, D_SSM), BF16),
        scratch_shapes=[pltpu.VMEM((seq, LANES), F32)],
        compiler_params=_params("parallel", "parallel"),
        name="s5_out",
    )(proj3, h_re, h_im, bk, mo_re, mo_im, dvec).reshape(t, D_SSM)

    tm, tn = min(512, t), 512
    nj = D_SSM // tn
    return pl.pallas_call(
        _glu_kernel,
        grid=(t // tm, nj),
        in_specs=[pl.BlockSpec((tm, D_SSM), lambda i, j: (i, 0)),
                  pl.BlockSpec((D_SSM, tn), lambda i, j: (0, j)),
                  pl.BlockSpec((D_SSM, tn), lambda i, j: (0, j + nj))],
        out_specs=pl.BlockSpec((tm, tn), lambda i, j: (i, j)),
        out_shape=jax.ShapeDtypeStruct((t, D_SSM), BF16),
        compiler_params=_params("parallel", "parallel"),
        name="s5_glu",
    )(y, w_glu, w_glu)


def _gates_kernel(x_ref, alog_ref, dtb_ref, o_ref):
    x = x_ref[...]
    lane = lax.broadcasted_iota(jnp.int32, x.shape, 1)
    xs = x + dtb_ref[...]
    softplus = jnp.maximum(xs, 0.0) + jnp.log1p(jnp.exp(-jnp.abs(xs)))
    o_ref[...] = jnp.where(lane < N_HEADS, _sigmoid(x), -jnp.exp(alog_ref[...]) * softplus)


def _dn_kernel(q_ref, k_ref, v_ref, z_ref, gt_ref, cw_ref, nw_ref, o_ref, st_s, x_s, qkv_s, *, nchunks):
    c, dh = DN_CHUNK, HEAD_DIM
    tb = nchunks * c

    @pl.when(pl.program_id(1) == 0)
    def _():
        st_s[...] = jnp.zeros_like(st_s)
        x_s[0:DN_HIST, :] = jnp.zeros((DN_HIST, x_s.shape[1]), F32)

    @pl.when(pl.program_id(1) > 0)
    def _():
        x_s[0:DN_HIST, :] = x_s[tb:tb + DN_HIST, :]

    for part, ref in enumerate((q_ref, k_ref, v_ref)):
        x_s[DN_HIST:DN_HIST + tb, part * D_DN:(part + 1) * D_DN] = ref[0]

    def conv(n, part, h):
        col = part * D_DN + h * dh
        x = x_s[pl.ds(pl.multiple_of(n * c, c), c + DN_HIST), col:col + dh]
        w = cw_ref[:, col:col + dh]
        acc = x * w[CONV_WIDTH - 1:CONV_WIDTH, :]
        for tap in range(CONV_WIDTH - 1):
            acc = acc + pltpu.roll(x, CONV_WIDTH - 1 - tap, axis=0) * w[tap:tap + 1, :]
        y = acc[DN_HIST:, :]
        y = y * _sigmoid(y)
        if part == 2:
            return y
        y = y * lax.rsqrt(jnp.sum(y * y, axis=-1, keepdims=True) + NORM_EPS)
        return y * (dh ** -0.5) if part == 0 else y

    def conv_chunk(n, slot):
        for part in range(3):
            for h in range(N_HEADS):
                col = part * D_DN + h * dh
                qkv_s[slot, :, col:col + dh] = conv(n, part, h)

    conv_chunk(0, 0)

    ri = lax.broadcasted_iota(jnp.int32, (c, c), 0)
    ci = lax.broadcasted_iota(jnp.int32, (c, c), 1)
    tril, strict = ri >= ci, ri > ci
    tril_f = tril.astype(F32)
    triu_f = (ri <= ci).astype(F32)
    eye = (ri == ci).astype(F32)
    bf = lambda x: x.astype(BF16)

    def chunk(n, carry):
        rows = pl.ds(pl.multiple_of(n * c, c), c)
        gt = gt_ref[0, rows, :]
        gcum_c = _dot(tril_f, gt, HI)
        gcum_r = _dot_tn(gt, triu_f, HI)
        heads = range(N_HEADS)
        cols = [slice(h * dh, (h + 1) * dh) for h in heads]
        beta = [gt[:, h:h + 1] for h in heads]
        gc = [gcum_c[:, N_HEADS + h:N_HEADS + h + 1] for h in heads]
        gr = [gcum_r[N_HEADS + h:N_HEADS + h + 1, :] for h in heads]
        gc_b = [jnp.broadcast_to(gc[h], (c, dh)) for h in heads]
        g_last = [gc_b[h][c - 1:c, :] for h in heads]
        decay = [jnp.where(tril, jnp.exp(jnp.where(tril, gc[h] - gr[h], 0.0)), 0.0) for h in heads]
        slot = n % 2
        q = [qkv_s[slot, :, cols[h]] for h in heads]
        k = [qkv_s[slot, :, D_DN + h * dh:D_DN + (h + 1) * dh] for h in heads]
        v = [qkv_s[slot, :, 2 * D_DN + h * dh:2 * D_DN + (h + 1) * dh] for h in heads]
        kb = [k[h] * beta[h] for h in heads]
        k16 = [bf(k[h]) for h in heads]
        lmat = [jnp.where(strict, _dot_nt(bf(kb[h]), k16[h]) * decay[h], 0.0) for h in heads]
        attn = [bf(jnp.where(tril, _dot_nt(bf(q[h]), k16[h]) * decay[h], 0.0)) for h in heads]
        tmat = [eye - lmat[h] for h in heads]
        lp16 = [bf(lmat[h]) for h in heads]
        lp16 = [bf(_dot(lp16[h], lp16[h])) for h in heads]
        for step in range(5):
            tmat = [tmat[h] + _dot(bf(tmat[h]), lp16[h]) for h in heads]
            if step < 4:
                lp16 = [bf(_dot(lp16[h], lp16[h])) for h in heads]
        t16 = [bf(tmat[h]) for h in heads]
        egc = [jnp.exp(gc_b[h]) for h in heads]
        w_val = [_dot(t16[h], bf(v[h] * beta[h])) for h in heads]
        k_cum = [bf(_dot(t16[h], bf(kb[h] * egc[h]))) for h in heads]
        state = [st_s[h] for h in heads]
        s16 = [bf(state[h]) for h in heads]
        out = [_dot(bf(q[h] * egc[h]), s16[h]) for h in heads]
        vn16 = [bf(w_val[h] - _dot(k_cum[h], s16[h])) for h in heads]
        out = [out[h] + _dot(attn[h], vn16[h]) for h in heads]
        for h in heads:
            k_tail = bf(k[h] * jnp.exp(g_last[h] - gc_b[h]))
            st_s[h] = state[h] * jnp.exp(g_last[h]) + _dot_tn(k_tail, vn16[h])
        for h in heads:
            o = out[h] * lax.rsqrt(jnp.mean(out[h] * out[h], axis=-1, keepdims=True) + NORM_EPS)
            z = z_ref[0, rows, cols[h]]
            o_ref[0, rows, cols[h]] = (o * nw_ref[...] * (z * _sigmoid(z))).astype(o_ref.dtype)
        conv_chunk(jnp.minimum(n + 1, nchunks - 1), 1 - slot)
        return carry

    lax.fori_loop(0, nchunks, chunk, 0)


def _dn_mixer(proj, gates_raw, bsz, seq, conv_w, a_log, dt_bias, norm_w):
    t = bsz * seq
    proj3 = proj.reshape(bsz, seq, N_MAIN)
    pad = jnp.zeros((1, LANES), F32)
    alog_p = pad.at[0, N_HEADS:2 * N_HEADS].set(a_log)
    dtb_p = pad.at[0, N_HEADS:2 * N_HEADS].set(dt_bias)
    tm = min(1024, t)
    gates = pl.pallas_call(
        _gates_kernel,
        grid=(t // tm,),
        in_specs=[pl.BlockSpec((tm, LANES), lambda i: (i, 0)),
                  pl.BlockSpec((1, LANES), lambda i: (0, 0)),
                  pl.BlockSpec((1, LANES), lambda i: (0, 0))],
        out_specs=pl.BlockSpec((tm, LANES), lambda i: (i, 0)),
        out_shape=jax.ShapeDtypeStruct((t, LANES), F32),
        compiler_params=_params("parallel"),
        name="dn_gates",
    )(gates_raw, alog_p, dtb_p).reshape(bsz, seq, LANES)

    tb = min(512, seq)
    part = lambda off: pl.BlockSpec((1, tb, D_DN), lambda b, i: (b, i, off))
    return pl.pallas_call(
        functools.partial(_dn_kernel, nchunks=tb // DN_CHUNK),
        grid=(bsz, seq // tb),
        in_specs=[part(1), part(2), part(3), part(4),
                  pl.BlockSpec((1, tb, LANES), lambda b, i: (b, i, 0)),
                  pl.BlockSpec((CONV_WIDTH, 3 * D_DN), lambda b, i: (0, 0)),
                  pl.BlockSpec((1, HEAD_DIM), lambda b, i: (0, 0))],
        out_specs=pl.BlockSpec((1, tb, D_DN), lambda b, i: (b, i, 0)),
        out_shape=jax.ShapeDtypeStruct((bsz, seq, D_DN), BF16),
        scratch_shapes=[pltpu.VMEM((N_HEADS, HEAD_DIM, HEAD_DIM), F32),
                        pltpu.VMEM((DN_HIST + tb, 3 * D_DN), F32),
                        pltpu.VMEM((2, DN_CHUNK, 3 * D_DN), F32)],
        compiler_params=_params("parallel", "arbitrary"),
        name="dn_chunk",
    )(proj3, proj3, proj3, proj3, gates, conv_w, norm_w.reshape(1, HEAD_DIM)).reshape(t, D_DN)


def _out_ln_kernel(a1_ref, a2_ref, w1_ref, w2_ref, x_ref, gt_ref, g_ref, b_ref, o_ref, *, alpha):
    y = _dot(a1_ref[...], w1_ref[...]) + _dot(a2_ref[...], w2_ref[...])
    r = alpha * x_ref[...] + gt_ref[0] * y
    o_ref[...] = _ln(r) * g_ref[...] + b_ref[...]


def _out_ln(y_ssm, y_dn, w_out, x, gt, gain, bias, seq, alpha):
    t, d = x.shape
    tm = min(256, seq)
    per = seq // tm
    return pl.pallas_call(
        functools.partial(_out_ln_kernel, alpha=alpha),
        grid=(t // tm,),
        in_specs=[pl.BlockSpec((tm, D_SSM), lambda i: (i, 0)),
                  pl.BlockSpec((tm, D_DN), lambda i: (i, 0)),
                  pl.BlockSpec((D_SSM, d), lambda i: (0, 0)),
                  pl.BlockSpec((D_DN, d), lambda i: (1, 0)),
                  pl.BlockSpec((tm, d), lambda i: (i, 0)),
                  pl.BlockSpec((1, 1, d), lambda i: (i // per, 0, 0)),
                  pl.BlockSpec((1, d), lambda i: (0, 0)),
                  pl.BlockSpec((1, d), lambda i: (0, 0))],
        out_specs=pl.BlockSpec((tm, d), lambda i: (i, 0)),
        out_shape=jax.ShapeDtypeStruct((t, d), F32),
        compiler_params=_params("parallel"),
        name="out_ln",
    )(y_ssm, y_dn, w_out, w_out, x, gt, gain.reshape(1, d), bias.reshape(1, d))


def _pack_rows(x):
    return pltpu.bitcast(x.astype(BF16), jnp.uint32)


def _unpack_rows(x):
    return pltpu.bitcast(x, BF16)


def _top_values(s, count):
    vals = []
    rank = jnp.full(s.shape, float(count), F32)
    for r in range(count):
        m = jnp.max(s, axis=0, keepdims=True)
        vals.append(m)
        hit = s == m
        rank = jnp.where(hit, float(r), rank)
        s = jnp.where(hit, -jnp.inf, s)
    return jnp.concatenate(vals, axis=0), rank


def _peer_route_kernel(ht_ref, wq_ref, keys_ref, a_ref, cnt_ref, b_ref, rank_ref, q_s):
    q_s[...] = _dot(wq_ref[...], ht_ref[...])
    k = PEER_TOPK

    def head(h, carry):
        base = pl.multiple_of(h * 2 * PEER_NKEYS, 2 * PEER_NKEYS)
        s0 = _dot(keys_ref[h, 0], q_s[pl.ds(base, PEER_NKEYS), :], HI)
        s1 = _dot(keys_ref[h, 1], q_s[pl.ds(base + PEER_NKEYS, PEER_NKEYS), :], HI)
        v0, rank0 = _top_values(s0, k)
        v1, rank1 = _top_values(s1, k)
        half = k // 2
        sub = lax.broadcasted_iota(jnp.int32, (half, v0.shape[1]), 0)
        cands = [v0[0:1] + v1[:half], v0[0:1] + v1[half:]]
        for x in range(1, half):
            cands.append(jnp.where(sub < k // (x + 1), v0[x:x + 1] + v1[:half], -jnp.inf))
        cands.append(v0[half:] + v1[0:1])
        top = v0[0:1] + v1[0:1]
        z = jnp.zeros_like(top)
        thr = top
        for r in range(k):
            m = cands[0]
            for cnd in cands[1:]:
                m = jnp.maximum(m, cnd)
            m = jnp.max(m, axis=0, keepdims=True)
            z = z + jnp.exp(m - top)
            thr = m
            if r + 1 < k:
                cands = [jnp.where(cnd == m, -jnp.inf, cnd) for cnd in cands]
        cnt_top = jnp.zeros_like(v0)
        for r in range(k):
            cnt_top = cnt_top + jnp.where(v0 + v1[r:r + 1] >= thr, 1.0, 0.0)
        cnt = jnp.zeros_like(s0)
        for x in range(k):
            cnt = jnp.where(rank0 == float(x), cnt_top[x:x + 1], cnt)
        a_ref[h] = jnp.exp(s0 - v0[0:1]) / z
        b_ref[h] = _pack_rows(jnp.exp(s1 - v1[0:1]))
        cnt_ref[h] = cnt
        rank_ref[h] = _pack_rows(rank1)
        return carry

    lax.fori_loop(0, PEER_HEADS, head, 0)


def _peer_dense_kernel(ht_ref, u_ref, vt_ref, a_ref, cnt_ref, b_ref, rank_ref, o_ref,
                       act_s, p_s, *, tl):
    e = pl.program_id(1)

    @pl.when(e == 0)
    def _():
        o_ref[...] = jnp.zeros_like(o_ref)

    first = pl.ds(pl.multiple_of(e * PEER_SUB, PEER_SUB), PEER_SUB)
    grp = PEER_PIECE * PEER_NKEYS
    npieces = PEER_TE // grp

    def up(pc):
        act_s[pc % 2] = _dot(u_ref[pc * grp:(pc + 1) * grp, :].astype(BF16), ht_ref[...])

    def down(pc):
        o_ref[...] += _dot(vt_ref[:, pc * grp:(pc + 1) * grp], p_s[pc % 2])

    def gate(pc):
        for lc in range(tl // LANES):
            ls = slice(lc * LANES, (lc + 1) * LANES)
            for j in range(PEER_PIECE):
                ii = PEER_PIECE * pc + j
                er = slice(j * PEER_NKEYS, (j + 1) * PEER_NKEYS)
                w = jnp.zeros((PEER_NKEYS, LANES), BF16)
                for h in range(PEER_HEADS):
                    row = lambda ref: jnp.broadcast_to(ref[h, first, ls][ii:ii + 1, :],
                                                       (PEER_NKEYS, LANES)).astype(BF16)
                    keep = _unpack_rows(rank_ref[h, :, ls]) < row(cnt_ref)
                    w = w + jnp.where(keep, _unpack_rows(b_ref[h, :, ls]), jnp.zeros((), BF16)) * row(a_ref)
                g = _gelu(act_s[pc % 2, er, ls]).astype(BF16)
                p_s[pc % 2, er, ls] = g * w

    up(0)
    for pc in range(npieces):
        if pc + 1 < npieces:
            up(pc + 1)
        if pc >= 1:
            down(pc - 1)
        gate(pc)
    down(npieces - 1)


def _peer_ffn(h_t, layer, wq_t, keys, u_all, vt_bf):
    d, t = h_t.shape
    nq = wq_t.shape[1]
    tl = min(256, t)
    gate = pl.BlockSpec((PEER_HEADS, PEER_NKEYS, tl), lambda i: (0, 0, i))
    pair = pl.BlockSpec((PEER_HEADS, PEER_NKEYS // 2, tl), lambda i: (0, 0, i))
    gshape = jax.ShapeDtypeStruct((PEER_HEADS, PEER_NKEYS, t), F32)
    pshape = jax.ShapeDtypeStruct((PEER_HEADS, PEER_NKEYS // 2, t), jnp.uint32)
    a, cnt, b, rank = pl.pallas_call(
        _peer_route_kernel,
        grid=(t // tl,),
        in_specs=[pl.BlockSpec((d, tl), lambda i: (0, i)),
                  pl.BlockSpec((None, nq, d), lambda i: (layer, 0, 0)),
                  pl.BlockSpec((PEER_HEADS, 2, PEER_NKEYS, PEER_NKEYS), lambda i: (0, 0, 0, 0))],
        out_specs=[gate, gate, pair, pair],
        out_shape=[gshape, gshape, pshape, pshape],
        scratch_shapes=[pltpu.VMEM((nq, tl), F32)],
        compiler_params=_params("parallel"),
        name="peer_route",
    )(h_t, wq_t, keys)

    tl, te = min(1024, t), PEER_TE
    once = pl.Buffered(1)
    gate = pl.BlockSpec((PEER_HEADS, PEER_NKEYS, tl), lambda i, e: (0, 0, i), pipeline_mode=once)
    pair = pl.BlockSpec((PEER_HEADS, PEER_NKEYS // 2, tl), lambda i, e: (0, 0, i), pipeline_mode=once)
    return pl.pallas_call(
        functools.partial(_peer_dense_kernel, tl=tl),
        grid=(t // tl, PEER_EXPERTS // te),
        in_specs=[pl.BlockSpec((d, tl), lambda i, e: (0, i), pipeline_mode=once),
                  pl.BlockSpec((None, te, d), lambda i, e: (layer, e, 0)),
                  pl.BlockSpec((None, None, d, te), lambda i, e: (layer, e, 0, 0)),
                  gate, gate, pair, pair],
        out_specs=pl.BlockSpec((d, tl), lambda i, e: (0, i), pipeline_mode=once),
        out_shape=jax.ShapeDtypeStruct((d, t), F32),
        scratch_shapes=[pltpu.VMEM((2, PEER_PIECE * PEER_NKEYS, tl), F32),
                        pltpu.VMEM((2, PEER_PIECE * PEER_NKEYS, tl), BF16)],
        compiler_params=_params("parallel", "arbitrary"),
        name="peer_dense",
    )(h_t, u_all, vt_bf, a, cnt, b, rank)


def kernel(x, c, w_ada, b_ada, w_in, ssm_lam_re, ssm_lam_im, ssm_log_step, ssm_b_re, ssm_b_im, ssm_c_re, ssm_c_im, ssm_d, ssm_w_glu, dn_conv_w, dn_a_log, dn_dt_bias, dn_norm_w, w_out, ln1_g, ln1_b, peer_w_query, peer_sub_keys, peer_u, peer_v, ln2_g, ln2_b):
    bsz, seq, d = x.shape
    depth = w_ada.shape[0]
    t = bsz * seq
    alpha = (2.0 * depth) ** 0.25

    mod = _ada_mod(c, w_ada, b_ada)[:, :bsz]
    w_in_bf = w_in.astype(BF16)
    w_gate = jnp.zeros((depth, d, LANES), BF16).at[:, :, :2 * N_HEADS].set(w_in_bf[:, :, N_MAIN:])
    wq_t = peer_w_query.transpose(0, 2, 1).astype(BF16)
    vt_bf = peer_v.reshape(depth, -1, PEER_TE, d).transpose(0, 1, 3, 2).astype(BF16)
    xf = x.reshape(t, d)
    for l in range(depth):
        sh1, sc1, gt1, sh2, sc2, gt2 = [m.reshape(bsz, 1, d) for m in jnp.split(mod[l], 6, axis=-1)]

        proj, gates_raw = _in_proj(xf, sh1, sc1, w_in_bf, w_gate, l, seq)

        prep = _s5_prep(ssm_lam_re[l], ssm_lam_im[l], ssm_log_step[l], ssm_b_re[l], ssm_b_im[l],
                        ssm_c_re[l], ssm_c_im[l], ssm_d[l])
        y_ssm = _s5_mixer(proj, bsz, seq, prep, ssm_w_glu[l].astype(BF16))
        y_dn = _dn_mixer(proj, gates_raw, bsz, seq, dn_conv_w[l], dn_a_log[l], dn_dt_bias[l],
                         dn_norm_w[l])
        xf = _out_ln(y_ssm, y_dn, w_out[l].astype(BF16), xf, gt1, ln1_g[l], ln1_b[l], seq, alpha)

        hffn_t = _ln_mod_t(xf, sh2, sc2, seq)
        y_t = _peer_ffn(hffn_t, l, wq_t, peer_sub_keys[l], peer_u, vt_bf)
        xf = _res_ln(xf, y_t, gt2, ln2_g[l], ln2_b[l], seq, alpha)
    return xf.reshape(bsz, seq, d)
```

```python
import functools
import math

import jax
import jax.numpy as jnp
from jax import lax
from jax.experimental import pallas as pl
from jax.experimental.pallas import tpu as pltpu

F32 = jnp.float32
BF16 = jnp.bfloat16
HI = lax.Precision.HIGHEST

D_SSM = 1024
SSM_GROUP = 16
N_GROUPS = D_SSM // SSM_GROUP
SSM_STATE = 64
SSM_CHUNK = 16
N_HEADS = 8
HEAD_DIM = 128
D_DN = N_HEADS * HEAD_DIM
DN_CHUNK = 64
CONV_WIDTH = 4
DN_HIST = 8
N_MAIN = D_SSM + 4 * D_DN
PEER_HEADS = 8
PEER_NKEYS = 128
PEER_EXPERTS = PEER_NKEYS * PEER_NKEYS
PEER_TOPK = 16
PEER_SUB = 8
PEER_TE = PEER_SUB * PEER_NKEYS
PEER_PIECE = 2
LN_EPS = 1e-5
NORM_EPS = 1e-6

LANES = 128
SUBLANES = 8
VMEM_LIMIT = 56 * 1024 * 1024


def _params(*sem):
    return pltpu.CompilerParams(dimension_semantics=sem, vmem_limit_bytes=VMEM_LIMIT)


def _gelu(x):
    return 0.5 * x * (1.0 + jnp.tanh(math.sqrt(2.0 / math.pi) * (x + 0.044715 * (x * x * x))))


def _sigmoid(x):
    return 1.0 / (1.0 + jnp.exp(-x))


def _dot(a, b, precision=None):
    return jnp.dot(a, b, precision=precision, preferred_element_type=F32)


def _dot_nt(a, b, precision=None):
    return lax.dot_general(a, b, (((1,), (1,)), ((), ())), precision=precision,
                           preferred_element_type=F32)


def _dot_tn(a, b, precision=None):
    return lax.dot_general(a, b, (((0,), (0,)), ((), ())), precision=precision,
                           preferred_element_type=F32)


def _ada_kernel(c_ref, w_ref, b_ref, o_ref):
    c = c_ref[...]
    ca = c * _sigmoid(c)
    hi = ca.astype(BF16)
    lo = (ca - hi.astype(F32)).astype(BF16)
    w = w_ref[0].astype(BF16)
    o_ref[0] = _dot(hi, w) + _dot(lo, w) + b_ref[0]


def _ada_mod(c, w_ada, b_ada):
    depth, d, n = w_ada.shape
    tn = 1024
    cp = jnp.zeros((SUBLANES, d), F32).at[: c.shape[0]].set(c)
    return pl.pallas_call(
        _ada_kernel,
        grid=(depth, n // tn),
        in_specs=[pl.BlockSpec((SUBLANES, d), lambda l, j: (0, 0)),
                  pl.BlockSpec((1, d, tn), lambda l, j: (l, 0, j)),
                  pl.BlockSpec((1, 1, tn), lambda l, j: (l, 0, j))],
        out_specs=pl.BlockSpec((1, SUBLANES, tn), lambda l, j: (l, 0, j)),
        out_shape=jax.ShapeDtypeStruct((depth, SUBLANES, n), F32),
        compiler_params=_params("parallel", "parallel"),
        name="ada_mod",
    )(cp, w_ada, b_ada.reshape(depth, 1, n))


def _ln(x):
    mu = jnp.mean(x, axis=-1, keepdims=True)
    xc = x - mu
    var = jnp.mean(xc * xc, axis=-1, keepdims=True)
    return xc * lax.rsqrt(var + LN_EPS)


def _ln_mod_kernel(x_ref, sh_ref, sc_ref, o_ref):
    y = _ln(x_ref[...]) * (1.0 + sc_ref[0]) + sh_ref[0]
    o_ref[...] = y.T.astype(o_ref.dtype)


def _ln_mod_t(x, sh, sc, seq):
    t, d = x.shape
    tm = min(256, seq)
    per = seq // tm
    return pl.pallas_call(
        _ln_mod_kernel,
        grid=(t // tm,),
        in_specs=[pl.BlockSpec((tm, d), lambda i: (i, 0)),
                  pl.BlockSpec((1, 1, d), lambda i: (i // per, 0, 0)),
                  pl.BlockSpec((1, 1, d), lambda i: (i // per, 0, 0))],
        out_specs=pl.BlockSpec((d, tm), lambda i: (0, i)),
        out_shape=jax.ShapeDtypeStruct((d, t), BF16),
        compiler_params=_params("parallel"),
        name="ln_mod",
    )(x, sh, sc)


def _res_ln_kernel(x_ref, yt_ref, gt_ref, g_ref, b_ref, o_ref, *, alpha):
    r = alpha * x_ref[...] + gt_ref[0] * yt_ref[...].T
    o_ref[...] = _ln(r) * g_ref[...] + b_ref[...]


def _res_ln(x, y_t, gt, gain, bias, seq, alpha):
    t, d = x.shape
    tm = min(256, seq)
    per = seq // tm
    return pl.pallas_call(
        functools.partial(_res_ln_kernel, alpha=alpha),
        grid=(t // tm,),
        in_specs=[pl.BlockSpec((tm, d), lambda i: (i, 0)),
                  pl.BlockSpec((d, tm), lambda i: (0, i)),
                  pl.BlockSpec((1, 1, d), lambda i: (i // per, 0, 0)),
                  pl.BlockSpec((1, d), lambda i: (0, 0)),
                  pl.BlockSpec((1, d), lambda i: (0, 0))],
        out_specs=pl.BlockSpec((tm, d), lambda i: (i, 0)),
        out_shape=jax.ShapeDtypeStruct((t, d), F32),
        compiler_params=_params("parallel"),
        name="res_ln",
    )(x, y_t, gt, gain.reshape(1, d), bias.reshape(1, d))


def _in_proj_kernel(x_ref, sh_ref, sc_ref, w_ref, wg_ref, o_ref, g_ref, hm_s):
    @pl.when(pl.program_id(1) == 0)
    def _():
        hm = (_ln(x_ref[...]) * (1.0 + sc_ref[0]) + sh_ref[0]).astype(BF16)
        hm_s[...] = hm
        g_ref[...] = _dot(hm, wg_ref[...])

    o_ref[...] = _dot(hm_s[...], w_ref[...])


def _in_proj(x, sh, sc, w_in_bf, w_gate, layer, seq):
    t, d = x.shape
    tm, tn = min(512, seq), 1024
    per = seq // tm
    return pl.pallas_call(
        _in_proj_kernel,
        grid=(t // tm, N_MAIN // tn),
        in_specs=[pl.BlockSpec((tm, d), lambda i, j: (i, 0)),
                  pl.BlockSpec((1, 1, d), lambda i, j: (i // per, 0, 0)),
                  pl.BlockSpec((1, 1, d), lambda i, j: (i // per, 0, 0)),
                  pl.BlockSpec((None, d, tn), lambda i, j: (layer, 0, j)),
                  pl.BlockSpec((None, d, LANES), lambda i, j: (layer, 0, 0))],
        out_specs=[pl.BlockSpec((tm, tn), lambda i, j: (i, j)),
                   pl.BlockSpec((tm, LANES), lambda i, j: (i, 0))],
        out_shape=[jax.ShapeDtypeStruct((t, N_MAIN), F32), jax.ShapeDtypeStruct((t, LANES), F32)],
        scratch_shapes=[pltpu.VMEM((tm, d), BF16)],
        compiler_params=_params("parallel", "arbitrary"),
        name="in_proj",
    )(x, sh, sc, w_in_bf, w_gate)


def _s5_prep(lam_re, lam_im, log_step, b_re, b_im, c_re, c_im, d_skip):
    g, p, nch = N_GROUPS, SSM_STATE, SSM_CHUNK
    gpt = LANES // SSM_GROUP
    ntile = g // gpt
    step = jnp.exp(log_step)[:, None]
    zr, zi = lam_re * step, lam_im * step
    ks = jnp.arange(nch + 1, dtype=F32)[:, None, None]
    mag = jnp.exp(ks * zr)
    pr, pi = mag * jnp.cos(ks * zi), mag * jnp.sin(ks * zi)
    nr, ni = pr[1] - 1.0, pi[1]
    den = lam_re * lam_re + lam_im * lam_im
    fr = (nr * lam_re + ni * lam_im) / den
    fi = (ni * lam_re - nr * lam_im) / den
    bbr = fr[..., None] * b_re - fi[..., None] * b_im
    bbi = fr[..., None] * b_im + fi[..., None] * b_re
    er = pr[:nch, :, :, None] * bbr - pi[:nch, :, :, None] * bbi
    ei = pr[:nch, :, :, None] * bbi + pi[:nch, :, :, None] * bbr
    kk = (jnp.einsum('gap,kgpc->kgac', c_re, er, precision=HI)
          - jnp.einsum('gap,kgpc->kgac', c_im, ei, precision=HI))
    wr = c_re[None] * pr[1:, :, None, :] - c_im[None] * pi[1:, :, None, :]
    wi = c_re[None] * pi[1:, :, None, :] + c_im[None] * pr[1:, :, None, :]

    def compact(m):
        m = m.reshape(nch, ntile, LANES, p).astype(BF16)
        return jnp.concatenate([m, m], axis=-1)

    bs_re, bs_im = compact(er[::-1].swapaxes(2, 3)), compact(ei[::-1].swapaxes(2, 3))
    mo_re, mo_im = compact(wr), compact(-wi)
    eye = jnp.eye(gpt, dtype=F32)[:, None, :, None]
    bk = kk.swapaxes(2, 3).reshape(nch, ntile, gpt, SSM_GROUP, 1, SSM_GROUP) * eye
    bk = bk.reshape(nch // 2, 2, ntile, LANES, LANES).astype(BF16)
    a_re, a_im = pr[nch].reshape(1, g * p), pi[nch].reshape(1, g * p)
    return bs_re, bs_im, bk, mo_re, mo_im, a_re, a_im, d_skip.reshape(1, D_SSM)


def _blockdiag_states(m, keep):
    reps = keep.shape[1] // m.shape[1]
    return jnp.where(keep, jnp.concatenate([m] * reps, axis=1), jnp.zeros((), m.dtype))


def _state_mask(width):
    rows = lax.broadcasted_iota(jnp.int32, (LANES, width), 0) // SSM_GROUP
    cols = lax.broadcasted_iota(jnp.int32, (LANES, width), 1) // SSM_STATE
    return rows == cols


def _s5_state_kernel(u_ref, bsr_ref, bsi_ref, sr_ref, si_ref):
    ncs = sr_ref.shape[0]
    keep = _state_mask(sr_ref.shape[1])
    acc_r = jnp.zeros(sr_ref.shape, F32)
    acc_i = jnp.zeros(si_ref.shape, F32)
    for s in range(SSM_CHUNK):
        xs = u_ref[0, pl.ds(s, ncs, stride=SSM_CHUNK), :].astype(BF16)
        acc_r = acc_r + _dot(xs, _blockdiag_states(bsr_ref[s], keep))
        acc_i = acc_i + _dot(xs, _blockdiag_states(bsi_ref[s], keep))
    sr_ref[...] = acc_r
    si_ref[...] = acc_i


def _s5_scan_kernel(sr_ref, si_ref, ar_ref, ai_ref, hr_ref, hi_ref):
    ar, ai = ar_ref[...], ai_ref[...]
    sub = SUBLANES

    def group(i, carry):
        hr, hi = carry
        rows = pl.ds(pl.multiple_of(i * sub, sub), sub)
        sr, si = sr_ref[rows, :], si_ref[rows, :]
        out_r, out_i = [], []
        for r in range(sub):
            out_r.append(hr)
            out_i.append(hi)
            hr, hi = ar * hr - ai * hi + sr[r:r + 1, :], ar * hi + ai * hr + si[r:r + 1, :]
        hr_ref[rows, :] = jnp.concatenate(out_r, axis=0)
        hi_ref[rows, :] = jnp.concatenate(out_i, axis=0)
        return hr, hi

    zero = jnp.zeros(ar.shape, F32)
    lax.fori_loop(0, sr_ref.shape[0] // sub, group, (zero, zero))


def _s5_out_kernel(u_ref, hr_ref, hi_ref, bk_ref, mor_ref, moi_ref, d_ref, o_ref, acc_s):
    x = u_ref[0]
    ncs = hr_ref.shape[0]
    pos = lax.broadcasted_iota(jnp.int32, x.shape, 0) % SSM_CHUNK

    def lagged(k):
        if k == 0:
            return x.astype(BF16)
        return jnp.where(pos >= k, pltpu.roll(x, k, axis=0), 0.0).astype(BF16)

    acc = d_ref[...] * x
    for kp in range(SSM_CHUNK // 2):
        xx = jnp.concatenate([lagged(2 * kp), lagged(2 * kp + 1)], axis=1)
        acc = acc + _dot(xx, bk_ref[kp].reshape(2 * LANES, LANES))
    acc_s[...] = acc
    hr, hi = hr_ref[...].astype(BF16), hi_ref[...].astype(BF16)
    keep = _state_mask(hr.shape[1])
    for r in range(SSM_CHUNK):
        rows = pl.ds(r, ncs, stride=SSM_CHUNK)
        acc_s[rows, :] = (acc_s[rows, :] + _dot_nt(hr, _blockdiag_states(mor_ref[r], keep))
                          + _dot_nt(hi, _blockdiag_states(moi_ref[r], keep)))
    o_ref[0] = _gelu(acc_s[...]).astype(o_ref.dtype)


def _glu_kernel(y_ref, wa_ref, wb_ref, o_ref):
    y = y_ref[...]
    o_ref[...] = (_dot(y, wa_ref[...]) * _sigmoid(_dot(y, wb_ref[...]))).astype(o_ref.dtype)


def _s5_mixer(proj, bsz, seq, prep, w_glu):
    bs_re, bs_im, bk, mo_re, mo_im, a_re, a_im, dvec = prep
    t = bsz * seq
    ncs = seq // SSM_CHUNK
    ntile = D_SSM // LANES
    sw = LANES // SSM_GROUP * SSM_STATE
    nstate = N_GROUPS * SSM_STATE
    proj3 = proj.reshape(bsz, seq, N_MAIN)
    u_spec = pl.BlockSpec((1, seq, LANES), lambda b, j: (b, 0, j))
    st_spec = pl.BlockSpec((ncs, sw), lambda b, j: (b, j))
    par = lambda m: pl.BlockSpec((m.shape[0], None, *m.shape[2:]), lambda b, j: (0, j, 0, 0))
    st_shape = jax.ShapeDtypeStruct((bsz * ncs, nstate), F32)

    s_re, s_im = pl.pallas_call(
        _s5_state_kernel,
        grid=(bsz, ntile),
        in_specs=[u_spec, par(bs_re), par(bs_im)],
        out_specs=[st_spec, st_spec],
        out_shape=[st_shape, st_shape],
        compiler_params=_params("parallel", "parallel"),
        name="s5_state",
    )(proj3, bs_re, bs_im)

    lb = 1024
    scan_spec = pl.BlockSpec((ncs, lb), lambda b, i: (b, i))
    coef_spec = pl.BlockSpec((1, lb), lambda b, i: (0, i))
    h_re, h_im = pl.pallas_call(
        _s5_scan_kernel,
        grid=(bsz, nstate // lb),
        in_specs=[scan_spec, scan_spec, coef_spec, coef_spec],
        out_specs=[scan_spec, scan_spec],
        out_shape=[st_shape, st_shape],
        compiler_params=_params("parallel", "parallel"),
        name="s5_scan",
    )(s_re, s_im, a_re, a_im)

    y = pl.pallas_call(
        _s5_out_kernel,
        grid=(bsz, ntile),
        in_specs=[u_spec, st_spec, st_spec,
                  pl.BlockSpec((*bk.shape[:2], None, LANES, LANES), lambda b, j: (0, 0, j, 0, 0)),
                  par(mo_re), par(mo_im), pl.BlockSpec((1, LANES), lambda b, j: (0, j))],
        out_specs=u_spec,
        out_shape=jax.ShapeDtypeStruct((bsz, seq, D_SSM), BF16),
        scratch_shapes=[pltpu.VMEM((seq, LANES), F32)],
        compiler_params=_params("parallel", "parallel"),
        name="s5_out",
    )(proj3, h_re, h_im, bk, mo_re, mo_im, dvec).reshape(t, D_SSM)

    tm, tn = min(512, t), 512
    nj = D_SSM // tn
    return pl.pallas_call(
        _glu_kernel,
        grid=(t // tm, nj),
        in_specs=[pl.BlockSpec((tm, D_SSM), lambda i, j: (i, 0)),
                  pl.BlockSpec((D_SSM, tn), lambda i, j: (0, j)),
                  pl.BlockSpec((D_SSM, tn), lambda i, j: (0, j + nj))],
        out_specs=pl.BlockSpec((tm, tn), lambda i, j: (i, j)),
        out_shape=jax.ShapeDtypeStruct((t, D_SSM), BF16),
        compiler_params=_params("parallel", "parallel"),
        name="s5_glu",
    )(y, w_glu, w_glu)


def _gates_kernel(x_ref, alog_ref, dtb_ref, o_ref):
    x = x_ref[...]
    lane = lax.broadcasted_iota(jnp.int32, x.shape, 1)
    xs = x + dtb_ref[...]
    softplus = jnp.maximum(xs, 0.0) + jnp.log1p(jnp.exp(-jnp.abs(xs)))
    o_ref[...] = jnp.where(lane < N_HEADS, _sigmoid(x), -jnp.exp(alog_ref[...]) * softplus)


def _dn_kernel(q_ref, k_ref, v_ref, z_ref, gt_ref, cw_ref, nw_ref, o_ref, st_s, x_s, qkv_s, *, nchunks):
    c, dh = DN_CHUNK, HEAD_DIM
    tb = nchunks * c

    @pl.when(pl.program_id(1) == 0)
    def _():
        st_s[...] = jnp.zeros_like(st_s)
        x_s[0:DN_HIST, :] = jnp.zeros((DN_HIST, x_s.shape[1]), F32)

    @pl.when(pl.program_id(1) > 0)
    def _():
        x_s[0:DN_HIST, :] = x_s[tb:tb + DN_HIST, :]

    for part, ref in enumerate((q_ref, k_ref, v_ref)):
        x_s[DN_HIST:DN_HIST + tb, part * D_DN:(part + 1) * D_DN] = ref[0]

    def conv(n, part, h):
        col = part * D_DN + h * dh
        x = x_s[pl.ds(pl.multiple_of(n * c, c), c + DN_HIST), col:col + dh]
        w = cw_ref[:, col:col + dh]
        acc = x * w[CONV_WIDTH - 1:CONV_WIDTH, :]
        for tap in range(CONV_WIDTH - 1):
            acc = acc + pltpu.roll(x, CONV_WIDTH - 1 - tap, axis=0) * w[tap:tap + 1, :]
        y = acc[DN_HIST:, :]
        y = y * _sigmoid(y)
        if part == 2:
            return y
        y = y * lax.rsqrt(jnp.sum(y * y, axis=-1, keepdims=True) + NORM_EPS)
        return y * (dh ** -0.5) if part == 0 else y

    def conv_chunk(n, slot):
        for part in range(3):
            for h in range(N_HEADS):
                col = part * D_DN + h * dh
                qkv_s[slot, :, col:col + dh] = conv(n, part, h)

    conv_chunk(0, 0)

    ri = lax.broadcasted_iota(jnp.int32, (c, c), 0)
    ci = lax.broadcasted_iota(jnp.int32, (c, c), 1)
    tril, strict = ri >= ci, ri > ci
    tril_f = tril.astype(F32)
    triu_f = (ri <= ci).astype(F32)
    eye = (ri == ci).astype(F32)
    bf = lambda x: x.astype(BF16)

    def chunk(n, carry):
        rows = pl.ds(pl.multiple_of(n * c, c), c)
        gt = gt_ref[0, rows, :]
        gcum_c = _dot(tril_f, gt, HI)
        gcum_r = _dot_tn(gt, triu_f, HI)
        heads = range(N_HEADS)
        cols = [slice(h * dh, (h + 1) * dh) for h in heads]
        beta = [gt[:, h:h + 1] for h in heads]
        gc = [gcum_c[:, N_HEADS + h:N_HEADS + h + 1] for h in heads]
        gr = [gcum_r[N_HEADS + h:N_HEADS + h + 1, :] for h in heads]
        gc_b = [jnp.broadcast_to(gc[h], (c, dh)) for h in heads]
        g_last = [gc_b[h][c - 1:c, :] for h in heads]
        decay = [jnp.where(tril, jnp.exp(jnp.where(tril, gc[h] - gr[h], 0.0)), 0.0) for h in heads]
        slot = n % 2
        q = [qkv_s[slot, :, cols[h]] for h in heads]
        k = [qkv_s[slot, :, D_DN + h * dh:D_DN + (h + 1) * dh] for h in heads]
        v = [qkv_s[slot, :, 2 * D_DN + h * dh:2 * D_DN + (h + 1) * dh] for h in heads]
        kb = [k[h] * beta[h] for h in heads]
        k16 = [bf(k[h]) for h in heads]
        lmat = [jnp.where(strict, _dot_nt(bf(kb[h]), k16[h]) * decay[h], 0.0) for h in heads]
        attn = [bf(jnp.where(tril, _dot_nt(bf(q[h]), k16[h]) * decay[h], 0.0)) for h in heads]
        tmat = [eye - lmat[h] for h in heads]
        lp16 = [bf(lmat[h]) for h in heads]
        lp16 = [bf(_dot(lp16[h], lp16[h])) for h in heads]
        for step in range(5):
            tmat = [tmat[h] + _dot(bf(tmat[h]), lp16[h]) for h in heads]
            if step < 4:
                lp16 = [bf(_dot(lp16[h], lp16[h])) for h in heads]
        t16 = [bf(tmat[h]) for h in heads]
        egc = [jnp.exp(gc_b[h]) for h in heads]
        w_val = [_dot(t16[h], bf(v[h] * beta[h])) for h in heads]
        k_cum = [bf(_dot(t16[h], bf(kb[h] * egc[h]))) for h in heads]
        state = [st_s[h] for h in heads]
        s16 = [bf(state[h]) for h in heads]
        out = [_dot(bf(q[h] * egc[h]), s16[h]) for h in heads]
        vn16 = [bf(w_val[h] - _dot(k_cum[h], s16[h])) for h in heads]
        out = [out[h] + _dot(attn[h], vn16[h]) for h in heads]
        for h in heads:
            k_tail = bf(k[h] * jnp.exp(g_last[h] - gc_b[h]))
            st_s[h] = state[h] * jnp.exp(g_last[h]) + _dot_tn(k_tail, vn16[h])
        for h in heads:
            o = out[h] * lax.rsqrt(jnp.mean(out[h] * out[h], axis=-1, keepdims=True) + NORM_EPS)
            z = z_ref[0, rows, cols[h]]
            o_ref[0, rows, cols[h]] = (o * nw_ref[...] * (z * _sigmoid(z))).astype(o_ref.dtype)
        conv_chunk(jnp.minimum(n + 1, nchunks - 1), 1 - slot)
        return carry

    lax.fori_loop(0, nchunks, chunk, 0)


def _dn_mixer(proj, gates_raw, bsz, seq, conv_w, a_log, dt_bias, norm_w):
    t = bsz * seq
    proj3 = proj.reshape(bsz, seq, N_MAIN)
    pad = jnp.zeros((1, LANES), F32)
    alog_p = pad.at[0, N_HEADS:2 * N_HEADS].set(a_log)
    dtb_p = pad.at[0, N_HEADS:2 * N_HEADS].set(dt_bias)
    tm = min(1024, t)
    gates = pl.pallas_call(
        _gates_kernel,
        grid=(t // tm,),
        in_specs=[pl.BlockSpec((tm, LANES), lambda i: (i, 0)),
                  pl.BlockSpec((1, LANES), lambda i: (0, 0)),
                  pl.BlockSpec((1, LANES), lambda i: (0, 0))],
        out_specs=pl.BlockSpec((tm, LANES), lambda i: (i, 0)),
        out_shape=jax.ShapeDtypeStruct((t, LANES), F32),
        compiler_params=_params("parallel"),
        name="dn_gates",
    )(gates_raw, alog_p, dtb_p).reshape(bsz, seq, LANES)

    tb = min(512, seq)
    part = lambda off: pl.BlockSpec((1, tb, D_DN), lambda b, i: (b, i, off))
    return pl.pallas_call(
        functools.partial(_dn_kernel, nchunks=tb // DN_CHUNK),
        grid=(bsz, seq // tb),
        in_specs=[part(1), part(2), part(3), part(4),
                  pl.BlockSpec((1, tb, LANES), lambda b, i: (b, i, 0)),
                  pl.BlockSpec((CONV_WIDTH, 3 * D_DN), lambda b, i: (0, 0)),
                  pl.BlockSpec((1, HEAD_DIM), lambda b, i: (0, 0))],
        out_specs=pl.BlockSpec((1, tb, D_DN), lambda b, i: (b, i, 0)),
        out_shape=jax.ShapeDtypeStruct((bsz, seq, D_DN), BF16),
        scratch_shapes=[pltpu.VMEM((N_HEADS, HEAD_DIM, HEAD_DIM), F32),
                        pltpu.VMEM((DN_HIST + tb, 3 * D_DN), F32),
                        pltpu.VMEM((2, DN_CHUNK, 3 * D_DN), F32)],
        compiler_params=_params("parallel", "arbitrary"),
        name="dn_chunk",
    )(proj3, proj3, proj3, proj3, gates, conv_w, norm_w.reshape(1, HEAD_DIM)).reshape(t, D_DN)


def _out_ln_kernel(a1_ref, a2_ref, w1_ref, w2_ref, x_ref, gt_ref, g_ref, b_ref, o_ref, *, alpha):
    y = _dot(a1_ref[...], w1_ref[...]) + _dot(a2_ref[...], w2_ref[...])
    r = alpha * x_ref[...] + gt_ref[0] * y
    o_ref[...] = _ln(r) * g_ref[...] + b_ref[...]


def _out_ln(y_ssm, y_dn, w_out, x, gt, gain, bias, seq, alpha):
    t, d = x.shape
    tm = min(256, seq)
    per = seq // tm
    return pl.pallas_call(
        functools.partial(_out_ln_kernel, alpha=alpha),
        grid=(t // tm,),
        in_specs=[pl.BlockSpec((tm, D_SSM), lambda i: (i, 0)),
                  pl.BlockSpec((tm, D_DN), lambda i: (i, 0)),
                  pl.BlockSpec((D_SSM, d), lambda i: (0, 0)),
                  pl.BlockSpec((D_DN, d), lambda i: (1, 0)),
                  pl.BlockSpec((tm, d), lambda i: (i, 0)),
                  pl.BlockSpec((1, 1, d), lambda i: (i // per, 0, 0)),
                  pl.BlockSpec((1, d), lambda i: (0, 0)),
                  pl.BlockSpec((1, d), lambda i: (0, 0))],
        out_specs=pl.BlockSpec((tm, d), lambda i: (i, 0)),
        out_shape=jax.ShapeDtypeStruct((t, d), F32),
        compiler_params=_params("parallel"),
        name="out_ln",
    )(y_ssm, y_dn, w_out, w_out, x, gt, gain.reshape(1, d), bias.reshape(1, d))


def _pack_rows(x):
    return pltpu.bitcast(x.astype(BF16), jnp.uint32)


def _unpack_rows(x):
    return pltpu.bitcast(x, BF16)


def _top_values(s, count):
    vals = []
    rank = jnp.full(s.shape, float(count), F32)
    for r in range(count):
        m = jnp.max(s, axis=0, keepdims=True)
        vals.append(m)
        hit = s == m
        rank = jnp.where(hit, float(r), rank)
        s = jnp.where(hit, -jnp.inf, s)
    return jnp.concatenate(vals, axis=0), rank


def _peer_route_kernel(ht_ref, wq_ref, keys_ref, a_ref, cnt_ref, b_ref, rank_ref, q_s):
    q_s[...] = _dot(wq_ref[...], ht_ref[...])
    k = PEER_TOPK

    def head(h, carry):
        base = pl.multiple_of(h * 2 * PEER_NKEYS, 2 * PEER_NKEYS)
        s0 = _dot(keys_ref[h, 0], q_s[pl.ds(base, PEER_NKEYS), :], HI)
        s1 = _dot(keys_ref[h, 1], q_s[pl.ds(base + PEER_NKEYS, PEER_NKEYS), :], HI)
        v0, rank0 = _top_values(s0, k)
        v1, rank1 = _top_values(s1, k)
        half = k // 2
        sub = lax.broadcasted_iota(jnp.int32, (half, v0.shape[1]), 0)
        cands = [v0[0:1] + v1[:half], v0[0:1] + v1[half:]]
        for x in range(1, half):
            cands.append(jnp.where(sub < k // (x + 1), v0[x:x + 1] + v1[:half], -jnp.inf))
        cands.append(v0[half:] + v1[0:1])
        top = v0[0:1] + v1[0:1]
        z = jnp.zeros_like(top)
        thr = top
        for r in range(k):
            m = cands[0]
            for cnd in cands[1:]:
                m = jnp.maximum(m, cnd)
            m = jnp.max(m, axis=0, keepdims=True)
            z = z + jnp.exp(m - top)
            thr = m
            if r + 1 < k:
                cands = [jnp.where(cnd == m, -jnp.inf, cnd) for cnd in cands]
        cnt_top = jnp.zeros_like(v0)
        for r in range(k):
            cnt_top = cnt_top + jnp.where(v0 + v1[r:r + 1] >= thr, 1.0, 0.0)
        cnt = jnp.zeros_like(s0)
        for x in range(k):
            cnt = jnp.where(rank0 == float(x), cnt_top[x:x + 1], cnt)
        a_ref[h] = jnp.exp(s0 - v0[0:1]) / z
        b_ref[h] = _pack_rows(jnp.exp(s1 - v1[0:1]))
        cnt_ref[h] = cnt
        rank_ref[h] = _pack_rows(rank1)
        return carry

    lax.fori_loop(0, PEER_HEADS, head, 0, unroll=4)


def _peer_dense_kernel(ht_ref, u_ref, vt_ref, a_ref, cnt_ref, b_ref, rank_ref, o_ref,
                       act_s, p_s, *, tl):
    e = pl.program_id(1)

    @pl.when(e == 0)
    def _():
        o_ref[...] = jnp.zeros_like(o_ref)

    first = pl.ds(pl.multiple_of(e * PEER_SUB, PEER_SUB), PEER_SUB)
    grp = PEER_PIECE * PEER_NKEYS
    npieces = PEER_TE // grp

    def up(pc):
        act_s[pc % 2] = _dot(u_ref[pc * grp:(pc + 1) * grp, :].astype(BF16), ht_ref[...])

    def down(pc):
        o_ref[...] += _dot(vt_ref[:, pc * grp:(pc + 1) * grp], p_s[pc % 2])

    def gate(pc):
        for lc in range(tl // LANES):
            ls = slice(lc * LANES, (lc + 1) * LANES)
            for j in range(PEER_PIECE):
                ii = PEER_PIECE * pc + j
                er = slice(j * PEER_NKEYS, (j + 1) * PEER_NKEYS)
                w = jnp.zeros((PEER_NKEYS, LANES), BF16)
                for h in range(PEER_HEADS):
                    row = lambda ref: jnp.broadcast_to(ref[h, first, ls][ii:ii + 1, :],
                                                       (PEER_NKEYS, LANES)).astype(BF16)
                    keep = _unpack_rows(rank_ref[h, :, ls]) < row(cnt_ref)
                    w = w + jnp.where(keep, _unpack_rows(b_ref[h, :, ls]), jnp.zeros((), BF16)) * row(a_ref)
                g = _gelu(act_s[pc % 2, er, ls]).astype(BF16)
                p_s[pc % 2, er, ls] = g * w

    up(0)
    for pc in range(npieces):
        if pc + 1 < npieces:
            up(pc + 1)
        if pc >= 1:
            down(pc - 1)
        gate(pc)
    down(npieces - 1)


def _peer_ffn(h_t, layer, wq_t, keys, u_all, vt_bf):
    d, t = h_t.shape
    nq = wq_t.shape[1]
    tl = min(256, t)
    gate = pl.BlockSpec((PEER_HEADS, PEER_NKEYS, tl), lambda i: (0, 0, i))
    pair = pl.BlockSpec((PEER_HEADS, PEER_NKEYS // 2, tl), lambda i: (0, 0, i))
    gshape = jax.ShapeDtypeStruct((PEER_HEADS, PEER_NKEYS, t), F32)
    pshape = jax.ShapeDtypeStruct((PEER_HEADS, PEER_NKEYS // 2, t), jnp.uint32)
    a, cnt, b, rank = pl.pallas_call(
        _peer_route_kernel,
        grid=(t // tl,),
        in_specs=[pl.BlockSpec((d, tl), lambda i: (0, i)),
                  pl.BlockSpec((None, nq, d), lambda i: (layer, 0, 0)),
                  pl.BlockSpec((PEER_HEADS, 2, PEER_NKEYS, PEER_NKEYS), lambda i: (0, 0, 0, 0))],
        out_specs=[gate, gate, pair, pair],
        out_shape=[gshape, gshape, pshape, pshape],
        scratch_shapes=[pltpu.VMEM((nq, tl), F32)],
        compiler_params=_params("parallel"),
        name="peer_route",
    )(h_t, wq_t, keys)

    tl, te = min(1024, t), PEER_TE
    once = pl.Buffered(1)
    gate = pl.BlockSpec((PEER_HEADS, PEER_NKEYS, tl), lambda i, e: (0, 0, i), pipeline_mode=once)
    pair = pl.BlockSpec((PEER_HEADS, PEER_NKEYS // 2, tl), lambda i, e: (0, 0, i), pipeline_mode=once)
    return pl.pallas_call(
        functools.partial(_peer_dense_kernel, tl=tl),
        grid=(t // tl, PEER_EXPERTS // te),
        in_specs=[pl.BlockSpec((d, tl), lambda i, e: (0, i), pipeline_mode=once),
                  pl.BlockSpec((None, te, d), lambda i, e: (layer, e, 0)),
                  pl.BlockSpec((None, None, d, te), lambda i, e: (layer, e, 0, 0)),
                  gate, gate, pair, pair],
        out_specs=pl.BlockSpec((d, tl), lambda i, e: (0, i), pipeline_mode=once),
        out_shape=jax.ShapeDtypeStruct((d, t), F32),
        scratch_shapes=[pltpu.VMEM((2, PEER_PIECE * PEER_NKEYS, tl), F32),
                        pltpu.VMEM((2, PEER_PIECE * PEER_NKEYS, tl), BF16)],
        compiler_params=_params("parallel", "arbitrary"),
        name="peer_dense",
    )(h_t, u_all, vt_bf, a, cnt, b, rank)


def kernel(x, c, w_ada, b_ada, w_in, ssm_lam_re, ssm_lam_im, ssm_log_step, ssm_b_re, ssm_b_im, ssm_c_re, ssm_c_im, ssm_d, ssm_w_glu, dn_conv_w, dn_a_log, dn_dt_bias, dn_norm_w, w_out, ln1_g, ln1_b, peer_w_query, peer_sub_keys, peer_u, peer_v, ln2_g, ln2_b):
    bsz, seq, d = x.shape
    depth = w_ada.shape[0]
    t = bsz * seq
    alpha = (2.0 * depth) ** 0.25

    mod = _ada_mod(c, w_ada, b_ada)[:, :bsz]
    w_in_bf = w_in.astype(BF16)
    w_gate = jnp.zeros((depth, d, LANES), BF16).at[:, :, :2 * N_HEADS].set(w_in_bf[:, :, N_MAIN:])
    wq_t = peer_w_query.transpose(0, 2, 1).astype(BF16)
    vt_bf = peer_v.reshape(depth, -1, PEER_TE, d).transpose(0, 1, 3, 2).astype(BF16)
    xf = x.reshape(t, d)
    for l in range(depth):
        sh1, sc1, gt1, sh2, sc2, gt2 = [m.reshape(bsz, 1, d) for m in jnp.split(mod[l], 6, axis=-1)]

        proj, gates_raw = _in_proj(xf, sh1, sc1, w_in_bf, w_gate, l, seq)

        prep = _s5_prep(ssm_lam_re[l], ssm_lam_im[l], ssm_log_step[l], ssm_b_re[l], ssm_b_im[l],
                        ssm_c_re[l], ssm_c_im[l], ssm_d[l])
        y_ssm = _s5_mixer(proj, bsz, seq, prep, ssm_w_glu[l].astype(BF16))
        y_dn = _dn_mixer(proj, gates_raw, bsz, seq, dn_conv_w[l], dn_a_log[l], dn_dt_bias[l],
                         dn_norm_w[l])
        xf = _out_ln(y_ssm, y_dn, w_out[l].astype(BF16), xf, gt1, ln1_g[l], ln1_b[l], seq, alpha)

        hffn_t = _ln_mod_t(xf, sh2, sc2, seq)
        y_t = _peer_ffn(hffn_t, l, wq_t, peer_sub_keys[l], peer_u, vt_bf)
        xf = _res_ln(xf, y_t, gt2, ln2_g[l], ln2_b[l], seq, alpha)
    return xf.reshape(bsz, seq, d)
```

```python
import functools
import math

import jax
import jax.numpy as jnp
from jax import lax
from jax.experimental import pallas as pl
from jax.experimental.pallas import tpu as pltpu

F32 = jnp.float32
BF16 = jnp.bfloat16
HI = lax.Precision.HIGHEST

D_SSM = 1024
SSM_GROUP = 16
N_GROUPS = D_SSM // SSM_GROUP
SSM_STATE = 64
SSM_CHUNK = 16
N_HEADS = 8
HEAD_DIM = 128
D_DN = N_HEADS * HEAD_DIM
DN_CHUNK = 64
CONV_WIDTH = 4
DN_HIST = 8
N_MAIN = D_SSM + 4 * D_DN
PEER_HEADS = 8
PEER_NKEYS = 128
PEER_EXPERTS = PEER_NKEYS * PEER_NKEYS
PEER_TOPK = 16
PEER_SUB = 8
PEER_TE = PEER_SUB * PEER_NKEYS
PEER_PIECE = 2
LN_EPS = 1e-5
NORM_EPS = 1e-6

LANES = 128
SUBLANES = 8
VMEM_LIMIT = 56 * 1024 * 1024


def _params(*sem):
    return pltpu.CompilerParams(dimension_semantics=sem, vmem_limit_bytes=VMEM_LIMIT)


def _gelu(x):
    return 0.5 * x * (1.0 + jnp.tanh(math.sqrt(2.0 / math.pi) * (x + 0.044715 * (x * x * x))))


def _sigmoid(x):
    return 1.0 / (1.0 + jnp.exp(-x))


def _dot(a, b, precision=None):
    return jnp.dot(a, b, precision=precision, preferred_element_type=F32)


def _dot_nt(a, b, precision=None):
    return lax.dot_general(a, b, (((1,), (1,)), ((), ())), precision=precision,
                           preferred_element_type=F32)


def _dot_tn(a, b, precision=None):
    return lax.dot_general(a, b, (((0,), (0,)), ((), ())), precision=precision,
                           preferred_element_type=F32)


def _ada_kernel(c_ref, w_ref, b_ref, o_ref):
    c = c_ref[...]
    ca = c * _sigmoid(c)
    hi = ca.astype(BF16)
    lo = (ca - hi.astype(F32)).astype(BF16)
    w = w_ref[0].astype(BF16)
    o_ref[0] = _dot(hi, w) + _dot(lo, w) + b_ref[0]


def _ada_mod(c, w_ada, b_ada):
    depth, d, n = w_ada.shape
    tn = 1024
    cp = jnp.zeros((SUBLANES, d), F32).at[: c.shape[0]].set(c)
    return pl.pallas_call(
        _ada_kernel,
        grid=(depth, n // tn),
        in_specs=[pl.BlockSpec((SUBLANES, d), lambda l, j: (0, 0)),
                  pl.BlockSpec((1, d, tn), lambda l, j: (l, 0, j)),
                  pl.BlockSpec((1, 1, tn), lambda l, j: (l, 0, j))],
        out_specs=pl.BlockSpec((1, SUBLANES, tn), lambda l, j: (l, 0, j)),
        out_shape=jax.ShapeDtypeStruct((depth, SUBLANES, n), F32),
        compiler_params=_params("parallel", "parallel"),
        name="ada_mod",
    )(cp, w_ada, b_ada.reshape(depth, 1, n))


def _ln(x):
    mu = jnp.mean(x, axis=-1, keepdims=True)
    xc = x - mu
    var = jnp.mean(xc * xc, axis=-1, keepdims=True)
    return xc * lax.rsqrt(var + LN_EPS)


def _ln_mod_kernel(x_ref, sh_ref, sc_ref, o_ref):
    y = _ln(x_ref[...]) * (1.0 + sc_ref[0]) + sh_ref[0]
    o_ref[...] = y.T.astype(o_ref.dtype)


def _ln_mod_t(x, sh, sc, seq):
    t, d = x.shape
    tm = min(256, seq)
    per = seq // tm
    return pl.pallas_call(
        _ln_mod_kernel,
        grid=(t // tm,),
        in_specs=[pl.BlockSpec((tm, d), lambda i: (i, 0)),
                  pl.BlockSpec((1, 1, d), lambda i: (i // per, 0, 0)),
                  pl.BlockSpec((1, 1, d), lambda i: (i // per, 0, 0))],
        out_specs=pl.BlockSpec((d, tm), lambda i: (0, i)),
        out_shape=jax.ShapeDtypeStruct((d, t), BF16),
        compiler_params=_params("parallel"),
        name="ln_mod",
    )(x, sh, sc)


def _res_ln_kernel(x_ref, yt_ref, gt_ref, g_ref, b_ref, o_ref, *, alpha):
    r = alpha * x_ref[...] + gt_ref[0] * yt_ref[...].T
    o_ref[...] = _ln(r) * g_ref[...] + b_ref[...]


def _res_ln(x, y_t, gt, gain, bias, seq, alpha):
    t, d = x.shape
    tm = min(256, seq)
    per = seq // tm
    return pl.pallas_call(
        functools.partial(_res_ln_kernel, alpha=alpha),
        grid=(t // tm,),
        in_specs=[pl.BlockSpec((tm, d), lambda i: (i, 0)),
                  pl.BlockSpec((d, tm), lambda i: (0, i)),
                  pl.BlockSpec((1, 1, d), lambda i: (i // per, 0, 0)),
                  pl.BlockSpec((1, d), lambda i: (0, 0)),
                  pl.BlockSpec((1, d), lambda i: (0, 0))],
        out_specs=pl.BlockSpec((tm, d), lambda i: (i, 0)),
        out_shape=jax.ShapeDtypeStruct((t, d), F32),
        compiler_params=_params("parallel"),
        name="res_ln",
    )(x, y_t, gt, gain.reshape(1, d), bias.reshape(1, d))


def _in_proj_kernel(x_ref, sh_ref, sc_ref, w_ref, wg_ref, o_ref, g_ref, hm_s):
    @pl.when(pl.program_id(1) == 0)
    def _():
        hm = (_ln(x_ref[...]) * (1.0 + sc_ref[0]) + sh_ref[0]).astype(BF16)
        hm_s[...] = hm
        g_ref[...] = _dot(hm, wg_ref[...])

    o_ref[...] = _dot(hm_s[...], w_ref[...])


def _in_proj(x, sh, sc, w_in_bf, w_gate, layer, seq):
    t, d = x.shape
    tm, tn = min(512, seq), 1024
    per = seq // tm
    return pl.pallas_call(
        _in_proj_kernel,
        grid=(t // tm, N_MAIN // tn),
        in_specs=[pl.BlockSpec((tm, d), lambda i, j: (i, 0)),
                  pl.BlockSpec((1, 1, d), lambda i, j: (i // per, 0, 0)),
                  pl.BlockSpec((1, 1, d), lambda i, j: (i // per, 0, 0)),
                  pl.BlockSpec((None, d, tn), lambda i, j: (layer, 0, j)),
                  pl.BlockSpec((None, d, LANES), lambda i, j: (layer, 0, 0))],
        out_specs=[pl.BlockSpec((tm, tn), lambda i, j: (i, j)),
                   pl.BlockSpec((tm, LANES), lambda i, j: (i, 0))],
        out_shape=[jax.ShapeDtypeStruct((t, N_MAIN), F32), jax.ShapeDtypeStruct((t, LANES), F32)],
        scratch_shapes=[pltpu.VMEM((tm, d), BF16)],
        compiler_params=_params("parallel", "arbitrary"),
        name="in_proj",
    )(x, sh, sc, w_in_bf, w_gate)


def _s5_prep(lam_re, lam_im, log_step, b_re, b_im, c_re, c_im, d_skip):
    g, p, nch = N_GROUPS, SSM_STATE, SSM_CHUNK
    gpt = LANES // SSM_GROUP
    ntile = g // gpt
    step = jnp.exp(log_step)[:, None]
    zr, zi = lam_re * step, lam_im * step
    ks = jnp.arange(nch + 1, dtype=F32)[:, None, None]
    mag = jnp.exp(ks * zr)
    pr, pi = mag * jnp.cos(ks * zi), mag * jnp.sin(ks * zi)
    nr, ni = pr[1] - 1.0, pi[1]
    den = lam_re * lam_re + lam_im * lam_im
    fr = (nr * lam_re + ni * lam_im) / den
    fi = (ni * lam_re - nr * lam_im) / den
    bbr = fr[..., None] * b_re - fi[..., None] * b_im
    bbi = fr[..., None] * b_im + fi[..., None] * b_re
    er = pr[:nch, :, :, None] * bbr - pi[:nch, :, :, None] * bbi
    ei = pr[:nch, :, :, None] * bbi + pi[:nch, :, :, None] * bbr
    kk = (jnp.einsum('gap,kgpc->kgac', c_re, er, precision=HI)
          - jnp.einsum('gap,kgpc->kgac', c_im, ei, precision=HI))
    wr = c_re[None] * pr[1:, :, None, :] - c_im[None] * pi[1:, :, None, :]
    wi = c_re[None] * pi[1:, :, None, :] + c_im[None] * pr[1:, :, None, :]

    def compact(m):
        m = m.reshape(nch, ntile, LANES, p).astype(BF16)
        return jnp.concatenate([m, m], axis=-1)

    bs_re, bs_im = compact(er[::-1].swapaxes(2, 3)), compact(ei[::-1].swapaxes(2, 3))
    mo_re, mo_im = compact(wr), compact(-wi)
    spread = jnp.tile(jnp.eye(SSM_GROUP, dtype=F32), (1, gpt))
    bk = jnp.einsum('kgca,ab->kgcb', kk.swapaxes(2, 3), spread, precision=HI)
    bk = bk.reshape(nch, ntile, LANES, LANES)
    lane_group = jnp.arange(LANES) // SSM_GROUP
    bk = jnp.where(lane_group[:, None] == lane_group[None, :], bk, 0.0)
    bk = bk.reshape(nch // 2, 2, ntile, LANES, LANES).astype(BF16)
    a_re, a_im = pr[nch].reshape(1, g * p), pi[nch].reshape(1, g * p)
    return bs_re, bs_im, bk, mo_re, mo_im, a_re, a_im, d_skip.reshape(1, D_SSM)


def _blockdiag_states(m, keep):
    reps = keep.shape[1] // m.shape[1]
    return jnp.where(keep, jnp.concatenate([m] * reps, axis=1), jnp.zeros((), m.dtype))


def _state_mask(width):
    rows = lax.broadcasted_iota(jnp.int32, (LANES, width), 0) // SSM_GROUP
    cols = lax.broadcasted_iota(jnp.int32, (LANES, width), 1) // SSM_STATE
    return rows == cols


def _s5_state_kernel(u_ref, bsr_ref, bsi_ref, sr_ref, si_ref):
    ncs = sr_ref.shape[0]
    keep = _state_mask(sr_ref.shape[1])
    acc_r = jnp.zeros(sr_ref.shape, F32)
    acc_i = jnp.zeros(si_ref.shape, F32)
    for s in range(SSM_CHUNK):
        xs = u_ref[0, pl.ds(s, ncs, stride=SSM_CHUNK), :].astype(BF16)
        acc_r = acc_r + _dot(xs, _blockdiag_states(bsr_ref[s], keep))
        acc_i = acc_i + _dot(xs, _blockdiag_states(bsi_ref[s], keep))
    sr_ref[...] = acc_r
    si_ref[...] = acc_i


def _s5_scan_kernel(sr_ref, si_ref, ar_ref, ai_ref, hr_ref, hi_ref):
    ar, ai = ar_ref[...], ai_ref[...]
    sub = SUBLANES

    def group(i, carry):
        hr, hi = carry
        rows = pl.ds(pl.multiple_of(i * sub, sub), sub)
        sr, si = sr_ref[rows, :], si_ref[rows, :]
        out_r, out_i = [], []
        for r in range(sub):
            out_r.append(hr)
            out_i.append(hi)
            hr, hi = ar * hr - ai * hi + sr[r:r + 1, :], ar * hi + ai * hr + si[r:r + 1, :]
        hr_ref[rows, :] = jnp.concatenate(out_r, axis=0)
        hi_ref[rows, :] = jnp.concatenate(out_i, axis=0)
        return hr, hi

    zero = jnp.zeros(ar.shape, F32)
    lax.fori_loop(0, sr_ref.shape[0] // sub, group, (zero, zero))


def _s5_out_kernel(u_ref, hr_ref, hi_ref, bk_ref, mor_ref, moi_ref, d_ref, o_ref, acc_s):
    x = u_ref[0]
    ncs = hr_ref.shape[0]
    pos = lax.broadcasted_iota(jnp.int32, x.shape, 0) % SSM_CHUNK

    def lagged(k):
        if k == 0:
            return x.astype(BF16)
        return jnp.where(pos >= k, pltpu.roll(x, k, axis=0), 0.0).astype(BF16)

    acc = d_ref[...] * x
    for kp in range(SSM_CHUNK // 2):
        xx = jnp.concatenate([lagged(2 * kp), lagged(2 * kp + 1)], axis=1)
        acc = acc + _dot(xx, bk_ref[kp].reshape(2 * LANES, LANES))
    acc_s[...] = acc
    hr, hi = hr_ref[...].astype(BF16), hi_ref[...].astype(BF16)
    keep = _state_mask(hr.shape[1])
    for r in range(SSM_CHUNK):
        rows = pl.ds(r, ncs, stride=SSM_CHUNK)
        acc_s[rows, :] = (acc_s[rows, :] + _dot_nt(hr, _blockdiag_states(mor_ref[r], keep))
                          + _dot_nt(hi, _blockdiag_states(moi_ref[r], keep)))
    o_ref[0] = _gelu(acc_s[...]).astype(o_ref.dtype)


def _glu_kernel(y_ref, wa_ref, wb_ref, o_ref):
    y = y_ref[...]
    o_ref[...] = (_dot(y, wa_ref[...]) * _sigmoid(_dot(y, wb_ref[...]))).astype(o_ref.dtype)


def _s5_mixer(proj, bsz, seq, prep, w_glu):
    bs_re, bs_im, bk, mo_re, mo_im, a_re, a_im, dvec = prep
    t = bsz * seq
    ncs = seq // SSM_CHUNK
    ntile = D_SSM // LANES
    sw = LANES // SSM_GROUP * SSM_STATE
    nstate = N_GROUPS * SSM_STATE
    proj3 = proj.reshape(bsz, seq, N_MAIN)
    u_spec = pl.BlockSpec((1, seq, LANES), lambda b, j: (b, 0, j))
    st_spec = pl.BlockSpec((ncs, sw), lambda b, j: (b, j))
    par = lambda m: pl.BlockSpec((m.shape[0], None, *m.shape[2:]), lambda b, j: (0, j, 0, 0))
    st_shape = jax.ShapeDtypeStruct((bsz * ncs, nstate), F32)

    s_re, s_im = pl.pallas_call(
        _s5_state_kernel,
        grid=(bsz, ntile),
        in_specs=[u_spec, par(bs_re), par(bs_im)],
        out_specs=[st_spec, st_spec],
        out_shape=[st_shape, st_shape],
        compiler_params=_params("parallel", "parallel"),
        name="s5_state",
    )(proj3, bs_re, bs_im)

    lb = 1024
    scan_spec = pl.BlockSpec((ncs, lb), lambda b, i: (b, i))
    coef_spec = pl.BlockSpec((1, lb), lambda b, i: (0, i))
    h_re, h_im = pl.pallas_call(
        _s5_scan_kernel,
        grid=(bsz, nstate // lb),
        in_specs=[scan_spec, scan_spec, coef_spec, coef_spec],
        out_specs=[scan_spec, scan_spec],
        out_shape=[st_shape, st_shape],
        compiler_params=_params("parallel", "parallel"),
        name="s5_scan",
    )(s_re, s_im, a_re, a_im)

    y = pl.pallas_call(
        _s5_out_kernel,
        grid=(bsz, ntile),
        in_specs=[u_spec, st_spec, st_spec,
                  pl.BlockSpec((*bk.shape[:2], None, LANES, LANES), lambda b, j: (0, 0, j, 0, 0)),
                  par(mo_re), par(mo_im), pl.BlockSpec((1, LANES), lambda b, j: (0, j))],
        out_specs=u_spec,
        out_shape=jax.ShapeDtypeStruct((bsz, seq, D_SSM), BF16),
        scratch_shapes=[pltpu.VMEM((seq, LANES), F32)],
        compiler_params=_params("parallel", "parallel"),
        name="s5_out",
    )(proj3, h_re, h_im, bk, mo_re, mo_im, dvec).reshape(t, D_SSM)

    tm, tn = min(512, t), 512
    nj = D_SSM // tn
    return pl.pallas_call(
        _glu_kernel,
        grid=(t // tm, nj),
        in_specs=[pl.BlockSpec((tm, D_SSM), lambda i, j: (i, 0)),
                  pl.BlockSpec((D_SSM, tn), lambda i, j: (0, j)),
                  pl.BlockSpec((D_SSM, tn), lambda i, j: (0, j + nj))],
        out_specs=pl.BlockSpec((tm, tn), lambda i, j: (i, j)),
        out_shape=jax.ShapeDtypeStruct((t, D_SSM), BF16),
        compiler_params=_params("parallel", "parallel"),
        name="s5_glu",
    )(y, w_glu, w_glu)


def _gates_kernel(x_ref, alog_ref, dtb_ref, o_ref):
    x = x_ref[...]
    lane = lax.broadcasted_iota(jnp.int32, x.shape, 1)
    xs = x + dtb_ref[...]
    softplus = jnp.maximum(xs, 0.0) + jnp.log1p(jnp.exp(-jnp.abs(xs)))
    o_ref[...] = jnp.where(lane < N_HEADS, _sigmoid(x), -jnp.exp(alog_ref[...]) * softplus)


def _dn_kernel(q_ref, k_ref, v_ref, z_ref, gt_ref, cw_ref, nw_ref, o_ref, st_s, x_s, qkv_s, *, nchunks):
    c, dh = DN_CHUNK, HEAD_DIM
    tb = nchunks * c

    @pl.when(pl.program_id(1) == 0)
    def _():
        st_s[...] = jnp.zeros_like(st_s)
        x_s[0:DN_HIST, :] = jnp.zeros((DN_HIST, x_s.shape[1]), F32)

    @pl.when(pl.program_id(1) > 0)
    def _():
        x_s[0:DN_HIST, :] = x_s[tb:tb + DN_HIST, :]

    for part, ref in enumerate((q_ref, k_ref, v_ref)):
        x_s[DN_HIST:DN_HIST + tb, part * D_DN:(part + 1) * D_DN] = ref[0]

    def conv(n, part, h):
        col = part * D_DN + h * dh
        x = x_s[pl.ds(pl.multiple_of(n * c, c), c + DN_HIST), col:col + dh]
        w = cw_ref[:, col:col + dh]
        acc = x * w[CONV_WIDTH - 1:CONV_WIDTH, :]
        for tap in range(CONV_WIDTH - 1):
            acc = acc + pltpu.roll(x, CONV_WIDTH - 1 - tap, axis=0) * w[tap:tap + 1, :]
        y = acc[DN_HIST:, :]
        y = y * _sigmoid(y)
        if part == 2:
            return y
        y = y * lax.rsqrt(jnp.sum(y * y, axis=-1, keepdims=True) + NORM_EPS)
        return y * (dh ** -0.5) if part == 0 else y

    def conv_chunk(n, slot):
        for part in range(3):
            for h in range(N_HEADS):
                col = part * D_DN + h * dh
                qkv_s[slot, :, col:col + dh] = conv(n, part, h)

    conv_chunk(0, 0)

    ri = lax.broadcasted_iota(jnp.int32, (c, c), 0)
    ci = lax.broadcasted_iota(jnp.int32, (c, c), 1)
    tril, strict = ri >= ci, ri > ci
    tril_f = tril.astype(F32)
    triu_f = (ri <= ci).astype(F32)
    eye = (ri == ci).astype(F32)
    bf = lambda x: x.astype(BF16)

    def chunk(n, carry):
        rows = pl.ds(pl.multiple_of(n * c, c), c)
        gt = gt_ref[0, rows, :]
        gcum_c = _dot(tril_f, gt, HI)
        gcum_r = _dot_tn(gt, triu_f, HI)
        heads = range(N_HEADS)
        cols = [slice(h * dh, (h + 1) * dh) for h in heads]
        beta = [gt[:, h:h + 1] for h in heads]
        gc = [gcum_c[:, N_HEADS + h:N_HEADS + h + 1] for h in heads]
        gr = [gcum_r[N_HEADS + h:N_HEADS + h + 1, :] for h in heads]
        gc_b = [jnp.broadcast_to(gc[h], (c, dh)) for h in heads]
        g_last = [gc_b[h][c - 1:c, :] for h in heads]
        decay = [jnp.where(tril, jnp.exp(jnp.where(tril, gc[h] - gr[h], 0.0)), 0.0) for h in heads]
        slot = n % 2
        q = [qkv_s[slot, :, cols[h]] for h in heads]
        k = [qkv_s[slot, :, D_DN + h * dh:D_DN + (h + 1) * dh] for h in heads]
        v = [qkv_s[slot, :, 2 * D_DN + h * dh:2 * D_DN + (h + 1) * dh] for h in heads]
        kb = [k[h] * beta[h] for h in heads]
        k16 = [bf(k[h]) for h in heads]
        lmat = [jnp.where(strict, _dot_nt(bf(kb[h]), k16[h]) * decay[h], 0.0) for h in heads]
        attn = [bf(jnp.where(tril, _dot_nt(bf(q[h]), k16[h]) * decay[h], 0.0)) for h in heads]
        tmat = [eye - lmat[h] for h in heads]
        lp16 = [bf(lmat[h]) for h in heads]
        lp16 = [bf(_dot(lp16[h], lp16[h])) for h in heads]
        for step in range(5):
            tmat = [tmat[h] + _dot(bf(tmat[h]), lp16[h]) for h in heads]
            if step < 4:
                lp16 = [bf(_dot(lp16[h], lp16[h])) for h in heads]
        t16 = [bf(tmat[h]) for h in heads]
        egc = [jnp.exp(gc_b[h]) for h in heads]
        w_val = [_dot(t16[h], bf(v[h] * beta[h])) for h in heads]
        k_cum = [bf(_dot(t16[h], bf(kb[h] * egc[h]))) for h in heads]
        state = [st_s[h] for h in heads]
        s16 = [bf(state[h]) for h in heads]
        out = [_dot(bf(q[h] * egc[h]), s16[h]) for h in heads]
        vn16 = [bf(w_val[h] - _dot(k_cum[h], s16[h])) for h in heads]
        out = [out[h] + _dot(attn[h], vn16[h]) for h in heads]
        for h in heads:
            k_tail = bf(k[h] * jnp.exp(g_last[h] - gc_b[h]))
            st_s[h] = state[h] * jnp.exp(g_last[h]) + _dot_tn(k_tail, vn16[h])
        for h in heads:
            o = out[h] * lax.rsqrt(jnp.mean(out[h] * out[h], axis=-1, keepdims=True) + NORM_EPS)
            z = z_ref[0, rows, cols[h]]
            o_ref[0, rows, cols[h]] = (o * nw_ref[...] * (z * _sigmoid(z))).astype(o_ref.dtype)
        conv_chunk(jnp.minimum(n + 1, nchunks - 1), 1 - slot)
        return carry

    lax.fori_loop(0, nchunks, chunk, 0)


def _dn_mixer(proj, gates_raw, bsz, seq, conv_w, a_log, dt_bias, norm_w):
    t = bsz * seq
    proj3 = proj.reshape(bsz, seq, N_MAIN)
    pad = jnp.zeros((1, LANES), F32)
    alog_p = pad.at[0, N_HEADS:2 * N_HEADS].set(a_log)
    dtb_p = pad.at[0, N_HEADS:2 * N_HEADS].set(dt_bias)
    tm = min(1024, t)
    gates = pl.pallas_call(
        _gates_kernel,
        grid=(t // tm,),
        in_specs=[pl.BlockSpec((tm, LANES), lambda i: (i, 0)),
                  pl.BlockSpec((1, LANES), lambda i: (0, 0)),
                  pl.BlockSpec((1, LANES), lambda i: (0, 0))],
        out_specs=pl.BlockSpec((tm, LANES), lambda i: (i, 0)),
        out_shape=jax.ShapeDtypeStruct((t, LANES), F32),
        compiler_params=_params("parallel"),
        name="dn_gates",
    )(gates_raw, alog_p, dtb_p).reshape(bsz, seq, LANES)

    tb = min(512, seq)
    part = lambda off: pl.BlockSpec((1, tb, D_DN), lambda b, i: (b, i, off))
    return pl.pallas_call(
        functools.partial(_dn_kernel, nchunks=tb // DN_CHUNK),
        grid=(bsz, seq // tb),
        in_specs=[part(1), part(2), part(3), part(4),
                  pl.BlockSpec((1, tb, LANES), lambda b, i: (b, i, 0)),
                  pl.BlockSpec((CONV_WIDTH, 3 * D_DN), lambda b, i: (0, 0)),
                  pl.BlockSpec((1, HEAD_DIM), lambda b, i: (0, 0))],
        out_specs=pl.BlockSpec((1, tb, D_DN), lambda b, i: (b, i, 0)),
        out_shape=jax.ShapeDtypeStruct((bsz, seq, D_DN), BF16),
        scratch_shapes=[pltpu.VMEM((N_HEADS, HEAD_DIM, HEAD_DIM), F32),
                        pltpu.VMEM((DN_HIST + tb, 3 * D_DN), F32),
                        pltpu.VMEM((2, DN_CHUNK, 3 * D_DN), F32)],
        compiler_params=_params("parallel", "arbitrary"),
        name="dn_chunk",
    )(proj3, proj3, proj3, proj3, gates, conv_w, norm_w.reshape(1, HEAD_DIM)).reshape(t, D_DN)


def _out_ln_kernel(a1_ref, a2_ref, w1_ref, w2_ref, x_ref, gt_ref, g_ref, b_ref, o_ref, *, alpha):
    y = _dot(a1_ref[...], w1_ref[...]) + _dot(a2_ref[...], w2_ref[...])
    r = alpha * x_ref[...] + gt_ref[0] * y
    o_ref[...] = _ln(r) * g_ref[...] + b_ref[...]


def _out_ln(y_ssm, y_dn, w_out, x, gt, gain, bias, seq, alpha):
    t, d = x.shape
    tm = min(256, seq)
    per = seq // tm
    return pl.pallas_call(
        functools.partial(_out_ln_kernel, alpha=alpha),
        grid=(t // tm,),
        in_specs=[pl.BlockSpec((tm, D_SSM), lambda i: (i, 0)),
                  pl.BlockSpec((tm, D_DN), lambda i: (i, 0)),
                  pl.BlockSpec((D_SSM, d), lambda i: (0, 0)),
                  pl.BlockSpec((D_DN, d), lambda i: (1, 0)),
                  pl.BlockSpec((tm, d), lambda i: (i, 0)),
                  pl.BlockSpec((1, 1, d), lambda i: (i // per, 0, 0)),
                  pl.BlockSpec((1, d), lambda i: (0, 0)),
                  pl.BlockSpec((1, d), lambda i: (0, 0))],
        out_specs=pl.BlockSpec((tm, d), lambda i: (i, 0)),
        out_shape=jax.ShapeDtypeStruct((t, d), F32),
        compiler_params=_params("parallel"),
        name="out_ln",
    )(y_ssm, y_dn, w_out, w_out, x, gt, gain.reshape(1, d), bias.reshape(1, d))


def _pack_rows(x):
    return pltpu.bitcast(x.astype(BF16), jnp.uint32)


def _unpack_rows(x):
    return pltpu.bitcast(x, BF16)


def _top_values(s, count):
    vals = []
    rank = jnp.full(s.shape, float(count), F32)
    for r in range(count):
        m = jnp.max(s, axis=0, keepdims=True)
        vals.append(m)
        hit = s == m
        rank = jnp.where(hit, float(r), rank)
        s = jnp.where(hit, -jnp.inf, s)
    return jnp.concatenate(vals, axis=0), rank


def _peer_route_kernel(ht_ref, wq_ref, keys_ref, a_ref, cnt_ref, b_ref, rank_ref, q_s):
    q_s[...] = _dot(wq_ref[...], ht_ref[...])
    k = PEER_TOPK

    def head(h, carry):
        base = pl.multiple_of(h * 2 * PEER_NKEYS, 2 * PEER_NKEYS)
        s0 = _dot(keys_ref[h, 0], q_s[pl.ds(base, PEER_NKEYS), :], HI)
        s1 = _dot(keys_ref[h, 1], q_s[pl.ds(base + PEER_NKEYS, PEER_NKEYS), :], HI)
        v0, rank0 = _top_values(s0, k)
        v1, rank1 = _top_values(s1, k)
        half = k // 2
        sub = lax.broadcasted_iota(jnp.int32, (half, v0.shape[1]), 0)
        cands = [v0[0:1] + v1[:half], v0[0:1] + v1[half:]]
        for x in range(1, half):
            cands.append(jnp.where(sub < k // (x + 1), v0[x:x + 1] + v1[:half], -jnp.inf))
        cands.append(v0[half:] + v1[0:1])
        top = v0[0:1] + v1[0:1]
        z = jnp.zeros_like(top)
        thr = top
        for r in range(k):
            m = cands[0]
            for cnd in cands[1:]:
                m = jnp.maximum(m, cnd)
            m = jnp.max(m, axis=0, keepdims=True)
            z = z + jnp.exp(m - top)
            thr = m
            if r + 1 < k:
                cands = [jnp.where(cnd == m, -jnp.inf, cnd) for cnd in cands]
        cnt_top = jnp.zeros_like(v0)
        for r in range(k):
            cnt_top = cnt_top + jnp.where(v0 + v1[r:r + 1] >= thr, 1.0, 0.0)
        cnt = jnp.zeros_like(s0)
        for x in range(k):
            cnt = jnp.where(rank0 == float(x), cnt_top[x:x + 1], cnt)
        a_ref[h] = jnp.exp(s0 - v0[0:1]) / z
        b_ref[h] = _pack_rows(jnp.exp(s1 - v1[0:1]))
        cnt_ref[h] = cnt
        rank_ref[h] = _pack_rows(rank1)
        return carry

    lax.fori_loop(0, PEER_HEADS, head, 0, unroll=4)


def _peer_dense_kernel(ht_ref, u_ref, vt_ref, a_ref, cnt_ref, b_ref, rank_ref, o_ref,
                       act_s, p_s, *, tl):
    e = pl.program_id(1)

    @pl.when(e == 0)
    def _():
        o_ref[...] = jnp.zeros_like(o_ref)

    first = pl.ds(pl.multiple_of(e * PEER_SUB, PEER_SUB), PEER_SUB)
    grp = PEER_PIECE * PEER_NKEYS
    npieces = PEER_TE // grp

    def up(pc):
        act_s[pc % 2] = _dot(u_ref[pc * grp:(pc + 1) * grp, :].astype(BF16), ht_ref[...])

    def down(pc):
        o_ref[...] += _dot(vt_ref[:, pc * grp:(pc + 1) * grp], p_s[pc % 2])

    def gate(pc):
        for lc in range(tl // LANES):
            ls = slice(lc * LANES, (lc + 1) * LANES)
            for j in range(PEER_PIECE):
                ii = PEER_PIECE * pc + j
                er = slice(j * PEER_NKEYS, (j + 1) * PEER_NKEYS)
                w = jnp.zeros((PEER_NKEYS, LANES), BF16)
                for h in range(PEER_HEADS):
                    row = lambda ref: jnp.broadcast_to(ref[h, first, ls][ii:ii + 1, :],
                                                       (PEER_NKEYS, LANES)).astype(BF16)
                    keep = _unpack_rows(rank_ref[h, :, ls]) < row(cnt_ref)
                    w = w + jnp.where(keep, _unpack_rows(b_ref[h, :, ls]), jnp.zeros((), BF16)) * row(a_ref)
                g = _gelu(act_s[pc % 2, er, ls]).astype(BF16)
                p_s[pc % 2, er, ls] = g * w

    up(0)
    for pc in range(npieces):
        if pc + 1 < npieces:
            up(pc + 1)
        if pc >= 1:
            down(pc - 1)
        gate(pc)
    down(npieces - 1)


def _peer_ffn(h_t, layer, wq_t, keys, u_all, vt_bf):
    d, t = h_t.shape
    nq = wq_t.shape[1]
    tl = min(256, t)
    gate = pl.BlockSpec((PEER_HEADS, PEER_NKEYS, tl), lambda i: (0, 0, i))
    pair = pl.BlockSpec((PEER_HEADS, PEER_NKEYS // 2, tl), lambda i: (0, 0, i))
    gshape = jax.ShapeDtypeStruct((PEER_HEADS, PEER_NKEYS, t), F32)
    pshape = jax.ShapeDtypeStruct((PEER_HEADS, PEER_NKEYS // 2, t), jnp.uint32)
    a, cnt, b, rank = pl.pallas_call(
        _peer_route_kernel,
        grid=(t // tl,),
        in_specs=[pl.BlockSpec((d, tl), lambda i: (0, i)),
                  pl.BlockSpec((None, nq, d), lambda i: (layer, 0, 0)),
                  pl.BlockSpec((PEER_HEADS, 2, PEER_NKEYS, PEER_NKEYS), lambda i: (0, 0, 0, 0))],
        out_specs=[gate, gate, pair, pair],
        out_shape=[gshape, gshape, pshape, pshape],
        scratch_shapes=[pltpu.VMEM((nq, tl), F32)],
        compiler_params=_params("parallel"),
        name="peer_route",
    )(h_t, wq_t, keys)

    tl, te = min(1024, t), PEER_TE
    once = pl.Buffered(1)
    gate = pl.BlockSpec((PEER_HEADS, PEER_NKEYS, tl), lambda i, e: (0, 0, i), pipeline_mode=once)
    pair = pl.BlockSpec((PEER_HEADS, PEER_NKEYS // 2, tl), lambda i, e: (0, 0, i), pipeline_mode=once)
    return pl.pallas_call(
        functools.partial(_peer_dense_kernel, tl=tl),
        grid=(t // tl, PEER_EXPERTS // te),
        in_specs=[pl.BlockSpec((d, tl), lambda i, e: (0, i), pipeline_mode=once),
                  pl.BlockSpec((None, te, d), lambda i, e: (layer, e, 0)),
                  pl.BlockSpec((None, None, d, te), lambda i, e: (layer, e, 0, 0)),
                  gate, gate, pair, pair],
        out_specs=pl.BlockSpec((d, tl), lambda i, e: (0, i), pipeline_mode=once),
        out_shape=jax.ShapeDtypeStruct((d, t), F32),
        scratch_shapes=[pltpu.VMEM((2, PEER_PIECE * PEER_NKEYS, tl), F32),
                        pltpu.VMEM((2, PEER_PIECE * PEER_NKEYS, tl), BF16)],
        compiler_params=_params("parallel", "arbitrary"),
        name="peer_dense",
    )(h_t, u_all, vt_bf, a, cnt, b, rank)


def kernel(x, c, w_ada, b_ada, w_in, ssm_lam_re, ssm_lam_im, ssm_log_step, ssm_b_re, ssm_b_im, ssm_c_re, ssm_c_im, ssm_d, ssm_w_glu, dn_conv_w, dn_a_log, dn_dt_bias, dn_norm_w, w_out, ln1_g, ln1_b, peer_w_query, peer_sub_keys, peer_u, peer_v, ln2_g, ln2_b):
    bsz, seq, d = x.shape
    depth = w_ada.shape[0]
    t = bsz * seq
    alpha = (2.0 * depth) ** 0.25

    mod = _ada_mod(c, w_ada, b_ada)[:, :bsz]
    w_in_bf = w_in.astype(BF16)
    w_gate = jnp.zeros((depth, d, LANES), BF16).at[:, :, :2 * N_HEADS].set(w_in_bf[:, :, N_MAIN:])
    wq_t = peer_w_query.transpose(0, 2, 1).astype(BF16)
    vt_bf = peer_v.reshape(depth, -1, PEER_TE, d).transpose(0, 1, 3, 2).astype(BF16)
    xf = x.reshape(t, d)
    for l in range(depth):
        sh1, sc1, gt1, sh2, sc2, gt2 = [m.reshape(bsz, 1, d) for m in jnp.split(mod[l], 6, axis=-1)]

        proj, gates_raw = _in_proj(xf, sh1, sc1, w_in_bf, w_gate, l, seq)

        prep = _s5_prep(ssm_lam_re[l], ssm_lam_im[l], ssm_log_step[l], ssm_b_re[l], ssm_b_im[l],
                        ssm_c_re[l], ssm_c_im[l], ssm_d[l])
        y_ssm = _s5_mixer(proj, bsz, seq, prep, ssm_w_glu[l].astype(BF16))
        y_dn = _dn_mixer(proj, gates_raw, bsz, seq, dn_conv_w[l], dn_a_log[l], dn_dt_bias[l],
                         dn_norm_w[l])
        xf = _out_ln(y_ssm, y_dn, w_out[l].astype(BF16), xf, gt1, ln1_g[l], ln1_b[l], seq, alpha)

        hffn_t = _ln_mod_t(xf, sh2, sc2, seq)
        y_t = _peer_ffn(hffn_t, l, wq_t, peer_sub_keys[l], peer_u, vt_bf)
        xf = _res_ln(xf, y_t, gt2, ln2_g[l], ln2_b[l], seq, alpha)
    return xf.reshape(bsz, seq, d)
```

```python
import functools
import math

import jax
import jax.numpy as jnp
from jax import lax
from jax.experimental import pallas as pl
from jax.experimental.pallas import tpu as pltpu

F32 = jnp.float32
BF16 = jnp.bfloat16
HI = lax.Precision.HIGHEST

D_SSM = 1024
SSM_GROUP = 16
N_GROUPS = D_SSM // SSM_GROUP
SSM_STATE = 64
SSM_CHUNK = 16
N_HEADS = 8
HEAD_DIM = 128
D_DN = N_HEADS * HEAD_DIM
DN_CHUNK = 64
CONV_WIDTH = 4
DN_HIST = 8
N_MAIN = D_SSM + 4 * D_DN
PEER_HEADS = 8
PEER_NKEYS = 128
PEER_EXPERTS = PEER_NKEYS * PEER_NKEYS
PEER_TOPK = 16
PEER_SUB = 8
PEER_TE = PEER_SUB * PEER_NKEYS
PEER_PIECE = 2
LN_EPS = 1e-5
NORM_EPS = 1e-6

LANES = 128
SUBLANES = 8
VMEM_LIMIT = 56 * 1024 * 1024


def _params(*sem):
    return pltpu.CompilerParams(dimension_semantics=sem, vmem_limit_bytes=VMEM_LIMIT)


def _gelu(x):
    return 0.5 * x * (1.0 + jnp.tanh(math.sqrt(2.0 / math.pi) * (x + 0.044715 * (x * x * x))))


def _sigmoid(x):
    return 1.0 / (1.0 + jnp.exp(-x))


def _dot(a, b, precision=None):
    return jnp.dot(a, b, precision=precision, preferred_element_type=F32)


def _dot_nt(a, b, precision=None):
    return lax.dot_general(a, b, (((1,), (1,)), ((), ())), precision=precision,
                           preferred_element_type=F32)


def _dot_tn(a, b, precision=None):
    return lax.dot_general(a, b, (((0,), (0,)), ((), ())), precision=precision,
                           preferred_element_type=F32)


def _ada_kernel(c_ref, w_ref, b_ref, o_ref):
    c = c_ref[...]
    ca = c * _sigmoid(c)
    hi = ca.astype(BF16)
    lo = (ca - hi.astype(F32)).astype(BF16)
    rows = ca.shape[0]
    r = _dot(jnp.concatenate([hi, lo], axis=0), w_ref[0].astype(BF16))
    o_ref[0] = r[:rows] + r[rows:] + b_ref[0]


def _ada_mod(c, w_ada, b_ada):
    depth, d, n = w_ada.shape
    tn = 1024
    cp = jnp.zeros((SUBLANES, d), F32).at[: c.shape[0]].set(c)
    return pl.pallas_call(
        _ada_kernel,
        grid=(depth, n // tn),
        in_specs=[pl.BlockSpec((SUBLANES, d), lambda l, j: (0, 0)),
                  pl.BlockSpec((1, d, tn), lambda l, j: (l, 0, j)),
                  pl.BlockSpec((1, 1, tn), lambda l, j: (l, 0, j))],
        out_specs=pl.BlockSpec((1, SUBLANES, tn), lambda l, j: (l, 0, j)),
        out_shape=jax.ShapeDtypeStruct((depth, SUBLANES, n), F32),
        compiler_params=_params("parallel", "parallel"),
        name="ada_mod",
    )(cp, w_ada, b_ada.reshape(depth, 1, n))


def _ln(x):
    mu = jnp.mean(x, axis=-1, keepdims=True)
    xc = x - mu
    var = jnp.mean(xc * xc, axis=-1, keepdims=True)
    return xc * lax.rsqrt(var + LN_EPS)


def _res_ln_kernel(x_ref, yt_ref, gt_ref, g_ref, b_ref, o_ref, *, alpha):
    r = alpha * x_ref[...] + gt_ref[0] * yt_ref[...].T
    o_ref[...] = _ln(r) * g_ref[...] + b_ref[...]


def _res_ln(x, y_t, gt, gain, bias, seq, alpha):
    t, d = x.shape
    tm = min(256, seq)
    per = seq // tm
    return pl.pallas_call(
        functools.partial(_res_ln_kernel, alpha=alpha),
        grid=(t // tm,),
        in_specs=[pl.BlockSpec((tm, d), lambda i: (i, 0)),
                  pl.BlockSpec((d, tm), lambda i: (0, i)),
                  pl.BlockSpec((1, 1, d), lambda i: (i // per, 0, 0)),
                  pl.BlockSpec((1, d), lambda i: (0, 0)),
                  pl.BlockSpec((1, d), lambda i: (0, 0))],
        out_specs=pl.BlockSpec((tm, d), lambda i: (i, 0)),
        out_shape=jax.ShapeDtypeStruct((t, d), F32),
        compiler_params=_params("parallel"),
        name="res_ln",
    )(x, y_t, gt, gain.reshape(1, d), bias.reshape(1, d))


def _in_proj_kernel(x_ref, sh_ref, sc_ref, w_ref, wg_ref, o_ref, g_ref, hm_s):
    @pl.when(pl.program_id(1) == 0)
    def _():
        hm = (_ln(x_ref[...]) * (1.0 + sc_ref[0]) + sh_ref[0]).astype(BF16)
        hm_s[...] = hm
        g_ref[...] = _dot(hm, wg_ref[...])

    o_ref[...] = _dot(hm_s[...], w_ref[...])


def _in_proj(x, sh, sc, w_in_bf, w_gate, layer, seq):
    t, d = x.shape
    tm, tn = min(512, seq), 1024
    per = seq // tm
    return pl.pallas_call(
        _in_proj_kernel,
        grid=(t // tm, N_MAIN // tn),
        in_specs=[pl.BlockSpec((tm, d), lambda i, j: (i, 0)),
                  pl.BlockSpec((1, 1, d), lambda i, j: (i // per, 0, 0)),
                  pl.BlockSpec((1, 1, d), lambda i, j: (i // per, 0, 0)),
                  pl.BlockSpec((None, d, tn), lambda i, j: (layer, 0, j)),
                  pl.BlockSpec((None, d, LANES), lambda i, j: (layer, 0, 0))],
        out_specs=[pl.BlockSpec((tm, tn), lambda i, j: (i, j)),
                   pl.BlockSpec((tm, LANES), lambda i, j: (i, 0))],
        out_shape=[jax.ShapeDtypeStruct((t, N_MAIN), F32), jax.ShapeDtypeStruct((t, LANES), F32)],
        scratch_shapes=[pltpu.VMEM((tm, d), BF16)],
        compiler_params=_params("parallel", "arbitrary"),
        name="in_proj",
    )(x, sh, sc, w_in_bf, w_gate)


def _s5_prep(lam_re, lam_im, log_step, b_re, b_im, c_re, c_im, d_skip):
    g, p, nch = N_GROUPS, SSM_STATE, SSM_CHUNK
    gpt = LANES // SSM_GROUP
    ntile = g // gpt
    step = jnp.exp(log_step)[:, None]
    zr, zi = lam_re * step, lam_im * step
    ks = jnp.arange(nch + 1, dtype=F32)[:, None, None]
    mag = jnp.exp(ks * zr)
    pr, pi = mag * jnp.cos(ks * zi), mag * jnp.sin(ks * zi)
    nr, ni = pr[1] - 1.0, pi[1]
    den = lam_re * lam_re + lam_im * lam_im
    fr = (nr * lam_re + ni * lam_im) / den
    fi = (ni * lam_re - nr * lam_im) / den
    bbr = fr[..., None] * b_re - fi[..., None] * b_im
    bbi = fr[..., None] * b_im + fi[..., None] * b_re
    er = pr[:nch, :, :, None] * bbr - pi[:nch, :, :, None] * bbi
    ei = pr[:nch, :, :, None] * bbi + pi[:nch, :, :, None] * bbr
    kk = (jnp.einsum('gap,kgpc->kgac', c_re, er, precision=HI)
          - jnp.einsum('gap,kgpc->kgac', c_im, ei, precision=HI))
    wr = c_re[None] * pr[1:, :, None, :] - c_im[None] * pi[1:, :, None, :]
    wi = c_re[None] * pi[1:, :, None, :] + c_im[None] * pr[1:, :, None, :]

    def compact(m):
        m = m.reshape(nch, ntile, LANES, p).astype(BF16)
        return jnp.concatenate([m, m], axis=-1)

    bs_re, bs_im = compact(er[::-1].swapaxes(2, 3)), compact(ei[::-1].swapaxes(2, 3))
    mo_re, mo_im = compact(wr), compact(-wi)
    spread = jnp.tile(jnp.eye(SSM_GROUP, dtype=F32), (1, gpt))
    bk = jnp.einsum('kgca,ab->kgcb', kk.swapaxes(2, 3), spread, precision=HI)
    bk = bk.reshape(nch, ntile, LANES, LANES)
    lane_group = jnp.arange(LANES) // SSM_GROUP
    bk = jnp.where(lane_group[:, None] == lane_group[None, :], bk, 0.0)
    bk = bk.reshape(nch // 2, 2, ntile, LANES, LANES).astype(BF16)
    a_re, a_im = pr[nch].reshape(1, g * p), pi[nch].reshape(1, g * p)
    return bs_re, bs_im, bk, mo_re, mo_im, a_re, a_im, d_skip.reshape(1, D_SSM)


def _blockdiag_states(m, keep):
    reps = keep.shape[1] // m.shape[1]
    return jnp.where(keep, jnp.concatenate([m] * reps, axis=1), jnp.zeros((), m.dtype))


def _state_mask(width):
    rows = lax.broadcasted_iota(jnp.int32, (LANES, width), 0) // SSM_GROUP
    cols = lax.broadcasted_iota(jnp.int32, (LANES, width), 1) // SSM_STATE
    return rows == cols


def _s5_state_kernel(u_ref, bsr_ref, bsi_ref, sr_ref, si_ref):
    ncs = sr_ref.shape[0]
    keep = _state_mask(sr_ref.shape[1])
    acc_r = jnp.zeros(sr_ref.shape, F32)
    acc_i = jnp.zeros(si_ref.shape, F32)
    for s in range(SSM_CHUNK):
        xs = u_ref[0, pl.ds(s, ncs, stride=SSM_CHUNK), :].astype(BF16)
        acc_r = acc_r + _dot(xs, _blockdiag_states(bsr_ref[s], keep))
        acc_i = acc_i + _dot(xs, _blockdiag_states(bsi_ref[s], keep))
    sr_ref[...] = acc_r
    si_ref[...] = acc_i


def _s5_scan_kernel(sr_ref, si_ref, ar_ref, ai_ref, hr_ref, hi_ref):
    ar, ai = ar_ref[...], ai_ref[...]
    sub = SUBLANES

    def group(i, carry):
        hr, hi = carry
        rows = pl.ds(pl.multiple_of(i * sub, sub), sub)
        sr, si = sr_ref[rows, :], si_ref[rows, :]
        out_r, out_i = [], []
        for r in range(sub):
            out_r.append(hr)
            out_i.append(hi)
            hr, hi = ar * hr - ai * hi + sr[r:r + 1, :], ar * hi + ai * hr + si[r:r + 1, :]
        hr_ref[rows, :] = jnp.concatenate(out_r, axis=0)
        hi_ref[rows, :] = jnp.concatenate(out_i, axis=0)
        return hr, hi

    zero = jnp.zeros(ar.shape, F32)
    lax.fori_loop(0, sr_ref.shape[0] // sub, group, (zero, zero))


def _s5_out_kernel(u_ref, hr_ref, hi_ref, bk_ref, mor_ref, moi_ref, d_ref, o_ref, acc_s):
    x = u_ref[0]
    ncs = hr_ref.shape[0]
    pos = lax.broadcasted_iota(jnp.int32, x.shape, 0) % SSM_CHUNK

    def lagged(k):
        if k == 0:
            return x.astype(BF16)
        return jnp.where(pos >= k, pltpu.roll(x, k, axis=0), 0.0).astype(BF16)

    acc = d_ref[...] * x
    for kp in range(SSM_CHUNK // 2):
        xx = jnp.concatenate([lagged(2 * kp), lagged(2 * kp + 1)], axis=1)
        acc = acc + _dot(xx, bk_ref[kp].reshape(2 * LANES, LANES))
    acc_s[...] = acc
    hr, hi = hr_ref[...].astype(BF16), hi_ref[...].astype(BF16)
    keep = _state_mask(hr.shape[1])
    for r in range(SSM_CHUNK):
        rows = pl.ds(r, ncs, stride=SSM_CHUNK)
        acc_s[rows, :] = (acc_s[rows, :] + _dot_nt(hr, _blockdiag_states(mor_ref[r], keep))
                          + _dot_nt(hi, _blockdiag_states(moi_ref[r], keep)))
    o_ref[0] = _gelu(acc_s[...]).astype(o_ref.dtype)


def _glu_kernel(y_ref, wa_ref, wb_ref, o_ref):
    y = y_ref[...]
    o_ref[...] = (_dot(y, wa_ref[...]) * _sigmoid(_dot(y, wb_ref[...]))).astype(o_ref.dtype)


def _s5_mixer(proj, bsz, seq, prep, w_glu):
    bs_re, bs_im, bk, mo_re, mo_im, a_re, a_im, dvec = prep
    t = bsz * seq
    ncs = seq // SSM_CHUNK
    ntile = D_SSM // LANES
    sw = LANES // SSM_GROUP * SSM_STATE
    nstate = N_GROUPS * SSM_STATE
    proj3 = proj.reshape(bsz, seq, N_MAIN)
    u_spec = pl.BlockSpec((1, seq, LANES), lambda b, j: (b, 0, j))
    st_spec = pl.BlockSpec((ncs, sw), lambda b, j: (b, j))
    par = lambda m: pl.BlockSpec((m.shape[0], None, *m.shape[2:]), lambda b, j: (0, j, 0, 0))
    st_shape = jax.ShapeDtypeStruct((bsz * ncs, nstate), F32)

    s_re, s_im = pl.pallas_call(
        _s5_state_kernel,
        grid=(bsz, ntile),
        in_specs=[u_spec, par(bs_re), par(bs_im)],
        out_specs=[st_spec, st_spec],
        out_shape=[st_shape, st_shape],
        compiler_params=_params("parallel", "parallel"),
        name="s5_state",
    )(proj3, bs_re, bs_im)

    lb = 1024
    scan_spec = pl.BlockSpec((ncs, lb), lambda b, i: (b, i))
    coef_spec = pl.BlockSpec((1, lb), lambda b, i: (0, i))
    h_re, h_im = pl.pallas_call(
        _s5_scan_kernel,
        grid=(bsz, nstate // lb),
        in_specs=[scan_spec, scan_spec, coef_spec, coef_spec],
        out_specs=[scan_spec, scan_spec],
        out_shape=[st_shape, st_shape],
        compiler_params=_params("parallel", "parallel"),
        name="s5_scan",
    )(s_re, s_im, a_re, a_im)

    y = pl.pallas_call(
        _s5_out_kernel,
        grid=(bsz, ntile),
        in_specs=[u_spec, st_spec, st_spec,
                  pl.BlockSpec((*bk.shape[:2], None, LANES, LANES), lambda b, j: (0, 0, j, 0, 0)),
                  par(mo_re), par(mo_im), pl.BlockSpec((1, LANES), lambda b, j: (0, j))],
        out_specs=u_spec,
        out_shape=jax.ShapeDtypeStruct((bsz, seq, D_SSM), BF16),
        scratch_shapes=[pltpu.VMEM((seq, LANES), F32)],
        compiler_params=_params("parallel", "parallel"),
        name="s5_out",
    )(proj3, h_re, h_im, bk, mo_re, mo_im, dvec).reshape(t, D_SSM)

    tm, tn = min(512, t), 512
    nj = D_SSM // tn
    return pl.pallas_call(
        _glu_kernel,
        grid=(t // tm, nj),
        in_specs=[pl.BlockSpec((tm, D_SSM), lambda i, j: (i, 0)),
                  pl.BlockSpec((D_SSM, tn), lambda i, j: (0, j)),
                  pl.BlockSpec((D_SSM, tn), lambda i, j: (0, j + nj))],
        out_specs=pl.BlockSpec((tm, tn), lambda i, j: (i, j)),
        out_shape=jax.ShapeDtypeStruct((t, D_SSM), BF16),
        compiler_params=_params("parallel", "parallel"),
        name="s5_glu",
    )(y, w_glu, w_glu)


def _gates_kernel(x_ref, alog_ref, dtb_ref, o_ref):
    x = x_ref[...]
    lane = lax.broadcasted_iota(jnp.int32, x.shape, 1)
    xs = x + dtb_ref[...]
    softplus = jnp.maximum(xs, 0.0) + jnp.log1p(jnp.exp(-jnp.abs(xs)))
    o_ref[...] = jnp.where(lane < N_HEADS, _sigmoid(x), -jnp.exp(alog_ref[...]) * softplus)


def _dn_kernel(q_ref, k_ref, v_ref, z_ref, gt_ref, cw_ref, nw_ref, o_ref, st_s, x_s, qkv_s, *, nchunks):
    c, dh = DN_CHUNK, HEAD_DIM
    tb = nchunks * c

    @pl.when(pl.program_id(1) == 0)
    def _():
        st_s[...] = jnp.zeros_like(st_s)
        x_s[0:DN_HIST, :] = jnp.zeros((DN_HIST, x_s.shape[1]), F32)

    @pl.when(pl.program_id(1) > 0)
    def _():
        x_s[0:DN_HIST, :] = x_s[tb:tb + DN_HIST, :]

    for part, ref in enumerate((q_ref, k_ref, v_ref)):
        x_s[DN_HIST:DN_HIST + tb, part * D_DN:(part + 1) * D_DN] = ref[0]

    def conv(n, part, h):
        col = part * D_DN + h * dh
        x = x_s[pl.ds(pl.multiple_of(n * c, c), c + DN_HIST), col:col + dh]
        w = cw_ref[:, col:col + dh]
        acc = x * w[CONV_WIDTH - 1:CONV_WIDTH, :]
        for tap in range(CONV_WIDTH - 1):
            acc = acc + pltpu.roll(x, CONV_WIDTH - 1 - tap, axis=0) * w[tap:tap + 1, :]
        y = acc[DN_HIST:, :]
        y = y * _sigmoid(y)
        if part == 2:
            return y
        y = y * lax.rsqrt(jnp.sum(y * y, axis=-1, keepdims=True) + NORM_EPS)
        return y * (dh ** -0.5) if part == 0 else y

    def conv_chunk(n, slot):
        for part in range(3):
            for h in range(N_HEADS):
                col = part * D_DN + h * dh
                qkv_s[slot, :, col:col + dh] = conv(n, part, h)

    conv_chunk(0, 0)

    ri = lax.broadcasted_iota(jnp.int32, (c, c), 0)
    ci = lax.broadcasted_iota(jnp.int32, (c, c), 1)
    tril, strict = ri >= ci, ri > ci
    tril_f = tril.astype(F32)
    triu_f = (ri <= ci).astype(F32)
    eye = (ri == ci).astype(F32)
    bf = lambda x: x.astype(BF16)

    def chunk(n, carry):
        rows = pl.ds(pl.multiple_of(n * c, c), c)
        gt = gt_ref[0, rows, :]
        gcum_c = _dot(tril_f, gt, HI)
        gcum_r = _dot_tn(gt, triu_f, HI)
        heads = range(N_HEADS)
        cols = [slice(h * dh, (h + 1) * dh) for h in heads]
        beta = [gt[:, h:h + 1] for h in heads]
        gc = [gcum_c[:, N_HEADS + h:N_HEADS + h + 1] for h in heads]
        gr = [gcum_r[N_HEADS + h:N_HEADS + h + 1, :] for h in heads]
        gc_b = [jnp.broadcast_to(gc[h], (c, dh)) for h in heads]
        g_last = [gc_b[h][c - 1:c, :] for h in heads]
        decay = [jnp.where(tril, jnp.exp(jnp.where(tril, gc[h] - gr[h], 0.0)), 0.0) for h in heads]
        slot = n % 2
        q = [qkv_s[slot, :, cols[h]] for h in heads]
        k = [qkv_s[slot, :, D_DN + h * dh:D_DN + (h + 1) * dh] for h in heads]
        v = [qkv_s[slot, :, 2 * D_DN + h * dh:2 * D_DN + (h + 1) * dh] for h in heads]
        kb = [k[h] * beta[h] for h in heads]
        k16 = [bf(k[h]) for h in heads]
        lmat = [jnp.where(strict, _dot_nt(bf(kb[h]), k16[h]) * decay[h], 0.0) for h in heads]
        attn = [bf(jnp.where(tril, _dot_nt(bf(q[h]), k16[h]) * decay[h], 0.0)) for h in heads]
        tmat = [eye - lmat[h] for h in heads]
        lp16 = [bf(lmat[h]) for h in heads]
        lp16 = [bf(_dot(lp16[h], lp16[h])) for h in heads]
        for step in range(5):
            tmat = [tmat[h] + _dot(bf(tmat[h]), lp16[h]) for h in heads]
            if step < 4:
                lp16 = [bf(_dot(lp16[h], lp16[h])) for h in heads]
        t16 = [bf(tmat[h]) for h in heads]
        egc = [jnp.exp(gc_b[h]) for h in heads]
        w_val = [_dot(t16[h], bf(v[h] * beta[h])) for h in heads]
        k_cum = [bf(_dot(t16[h], bf(kb[h] * egc[h]))) for h in heads]
        state = [st_s[h] for h in heads]
        s16 = [bf(state[h]) for h in heads]
        out = [_dot(bf(q[h] * egc[h]), s16[h]) for h in heads]
        vn16 = [bf(w_val[h] - _dot(k_cum[h], s16[h])) for h in heads]
        out = [out[h] + _dot(attn[h], vn16[h]) for h in heads]
        for h in heads:
            k_tail = bf(k[h] * jnp.exp(g_last[h] - gc_b[h]))
            st_s[h] = state[h] * jnp.exp(g_last[h]) + _dot_tn(k_tail, vn16[h])
        for h in heads:
            o = out[h] * lax.rsqrt(jnp.mean(out[h] * out[h], axis=-1, keepdims=True) + NORM_EPS)
            z = z_ref[0, rows, cols[h]]
            o_ref[0, rows, cols[h]] = (o * nw_ref[...] * (z * _sigmoid(z))).astype(o_ref.dtype)
        conv_chunk(jnp.minimum(n + 1, nchunks - 1), 1 - slot)
        return carry

    lax.fori_loop(0, nchunks, chunk, 0)


def _dn_mixer(proj, gates_raw, bsz, seq, conv_w, a_log, dt_bias, norm_w):
    t = bsz * seq
    proj3 = proj.reshape(bsz, seq, N_MAIN)
    pad = jnp.zeros((1, LANES), F32)
    alog_p = pad.at[0, N_HEADS:2 * N_HEADS].set(a_log)
    dtb_p = pad.at[0, N_HEADS:2 * N_HEADS].set(dt_bias)
    tm = min(1024, t)
    gates = pl.pallas_call(
        _gates_kernel,
        grid=(t // tm,),
        in_specs=[pl.BlockSpec((tm, LANES), lambda i: (i, 0)),
                  pl.BlockSpec((1, LANES), lambda i: (0, 0)),
                  pl.BlockSpec((1, LANES), lambda i: (0, 0))],
        out_specs=pl.BlockSpec((tm, LANES), lambda i: (i, 0)),
        out_shape=jax.ShapeDtypeStruct((t, LANES), F32),
        compiler_params=_params("parallel"),
        name="dn_gates",
    )(gates_raw, alog_p, dtb_p).reshape(bsz, seq, LANES)

    tb = min(512, seq)
    part = lambda off: pl.BlockSpec((1, tb, D_DN), lambda b, i: (b, i, off))
    return pl.pallas_call(
        functools.partial(_dn_kernel, nchunks=tb // DN_CHUNK),
        grid=(bsz, seq // tb),
        in_specs=[part(1), part(2), part(3), part(4),
                  pl.BlockSpec((1, tb, LANES), lambda b, i: (b, i, 0)),
                  pl.BlockSpec((CONV_WIDTH, 3 * D_DN), lambda b, i: (0, 0)),
                  pl.BlockSpec((1, HEAD_DIM), lambda b, i: (0, 0))],
        out_specs=pl.BlockSpec((1, tb, D_DN), lambda b, i: (b, i, 0)),
        out_shape=jax.ShapeDtypeStruct((bsz, seq, D_DN), BF16),
        scratch_shapes=[pltpu.VMEM((N_HEADS, HEAD_DIM, HEAD_DIM), F32),
                        pltpu.VMEM((DN_HIST + tb, 3 * D_DN), F32),
                        pltpu.VMEM((2, DN_CHUNK, 3 * D_DN), F32)],
        compiler_params=_params("parallel", "arbitrary"),
        name="dn_chunk",
    )(proj3, proj3, proj3, proj3, gates, conv_w, norm_w.reshape(1, HEAD_DIM)).reshape(t, D_DN)


def _out_ln_kernel(a1_ref, a2_ref, w1_ref, w2_ref, x_ref, gt_ref, g_ref, b_ref, sh_ref, sc_ref,
                   o_ref, ht_ref, *, alpha):
    y = _dot(a1_ref[...], w1_ref[...]) + _dot(a2_ref[...], w2_ref[...])
    r = alpha * x_ref[...] + gt_ref[0] * y
    x_new = _ln(r) * g_ref[...] + b_ref[...]
    o_ref[...] = x_new
    ht_ref[...] = (_ln(x_new) * (1.0 + sc_ref[0]) + sh_ref[0]).T.astype(ht_ref.dtype)


def _out_ln(y_ssm, y_dn, w_out, x, gt, gain, bias, sh, sc, seq, alpha):
    t, d = x.shape
    tm = min(256, seq)
    per = seq // tm
    return pl.pallas_call(
        functools.partial(_out_ln_kernel, alpha=alpha),
        grid=(t // tm,),
        in_specs=[pl.BlockSpec((tm, D_SSM), lambda i: (i, 0)),
                  pl.BlockSpec((tm, D_DN), lambda i: (i, 0)),
                  pl.BlockSpec((D_SSM, d), lambda i: (0, 0)),
                  pl.BlockSpec((D_DN, d), lambda i: (1, 0)),
                  pl.BlockSpec((tm, d), lambda i: (i, 0)),
                  pl.BlockSpec((1, 1, d), lambda i: (i // per, 0, 0)),
                  pl.BlockSpec((1, d), lambda i: (0, 0)),
                  pl.BlockSpec((1, d), lambda i: (0, 0)),
                  pl.BlockSpec((1, 1, d), lambda i: (i // per, 0, 0)),
                  pl.BlockSpec((1, 1, d), lambda i: (i // per, 0, 0))],
        out_specs=[pl.BlockSpec((tm, d), lambda i: (i, 0)),
                   pl.BlockSpec((d, tm), lambda i: (0, i))],
        out_shape=[jax.ShapeDtypeStruct((t, d), F32), jax.ShapeDtypeStruct((d, t), BF16)],
        compiler_params=_params("parallel"),
        name="out_ln",
    )(y_ssm, y_dn, w_out, w_out, x, gt, gain.reshape(1, d), bias.reshape(1, d), sh, sc)


def _pack_rows(x):
    return pltpu.bitcast(x.astype(BF16), jnp.uint32)


def _unpack_rows(x):
    return pltpu.bitcast(x, BF16)


def _top_values(s, count):
    vals = []
    rank = jnp.full(s.shape, float(count), F32)
    for r in range(count):
        m = jnp.max(s, axis=0, keepdims=True)
        vals.append(m)
        hit = s == m
        rank = jnp.where(hit, float(r), rank)
        s = jnp.where(hit, -jnp.inf, s)
    return jnp.concatenate(vals, axis=0), rank


def _peer_route_kernel(ht_ref, wq_ref, keys_ref, a_ref, cnt_ref, b_ref, rank_ref, q_s):
    q_s[...] = _dot(wq_ref[...], ht_ref[...])
    k = PEER_TOPK

    def head(h, carry):
        base = pl.multiple_of(h * 2 * PEER_NKEYS, 2 * PEER_NKEYS)
        s0 = _dot(keys_ref[h, 0], q_s[pl.ds(base, PEER_NKEYS), :], HI)
        s1 = _dot(keys_ref[h, 1], q_s[pl.ds(base + PEER_NKEYS, PEER_NKEYS), :], HI)
        v0, rank0 = _top_values(s0, k)
        v1, rank1 = _top_values(s1, k)
        half = k // 2
        sub = lax.broadcasted_iota(jnp.int32, (half, v0.shape[1]), 0)
        cands = [v0[0:1] + v1[:half], v0[0:1] + v1[half:]]
        for x in range(1, half):
            cands.append(jnp.where(sub < k // (x + 1), v0[x:x + 1] + v1[:half], -jnp.inf))
        cands.append(v0[half:] + v1[0:1])
        top = v0[0:1] + v1[0:1]
        z = jnp.zeros_like(top)
        thr = top
        for r in range(k):
            m = cands[0]
            for cnd in cands[1:]:
                m = jnp.maximum(m, cnd)
            m = jnp.max(m, axis=0, keepdims=True)
            z = z + jnp.exp(m - top)
            thr = m
            if r + 1 < k:
                cands = [jnp.where(cnd == m, -jnp.inf, cnd) for cnd in cands]
        cnt_top = jnp.zeros_like(v0)
        for r in range(k):
            cnt_top = cnt_top + jnp.where(v0 + v1[r:r + 1] >= thr, 1.0, 0.0)
        cnt = jnp.zeros_like(s0)
        for x in range(k):
            cnt = jnp.where(rank0 == float(x), cnt_top[x:x + 1], cnt)
        a_ref[h] = jnp.exp(s0 - v0[0:1]) / z
        b_ref[h] = _pack_rows(jnp.exp(s1 - v1[0:1]))
        cnt_ref[h] = cnt
        rank_ref[h] = _pack_rows(rank1)
        return carry

    lax.fori_loop(0, PEER_HEADS, head, 0, unroll=4)


def _peer_dense_kernel(ht_ref, u_ref, vt_ref, a_ref, cnt_ref, b_ref, rank_ref, o_ref,
                       act_s, p_s, *, tl):
    e = pl.program_id(1)

    @pl.when(e == 0)
    def _():
        o_ref[...] = jnp.zeros_like(o_ref)

    first = pl.ds(pl.multiple_of(e * PEER_SUB, PEER_SUB), PEER_SUB)
    grp = PEER_PIECE * PEER_NKEYS
    npieces = PEER_TE // grp

    def up(pc):
        act_s[pc % 2] = _dot(u_ref[pc * grp:(pc + 1) * grp, :].astype(BF16), ht_ref[...])

    def down(pc):
        o_ref[...] += _dot(vt_ref[:, pc * grp:(pc + 1) * grp], p_s[pc % 2])

    def gate(pc):
        for lc in range(tl // LANES):
            ls = slice(lc * LANES, (lc + 1) * LANES)
            for j in range(PEER_PIECE):
                ii = PEER_PIECE * pc + j
                er = slice(j * PEER_NKEYS, (j + 1) * PEER_NKEYS)
                w = jnp.zeros((PEER_NKEYS, LANES), BF16)
                for h in range(PEER_HEADS):
                    row = lambda ref: jnp.broadcast_to(ref[h, first, ls][ii:ii + 1, :],
                                                       (PEER_NKEYS, LANES)).astype(BF16)
                    keep = _unpack_rows(rank_ref[h, :, ls]) < row(cnt_ref)
                    w = w + jnp.where(keep, _unpack_rows(b_ref[h, :, ls]), jnp.zeros((), BF16)) * row(a_ref)
                g = _gelu(act_s[pc % 2, er, ls]).astype(BF16)
                p_s[pc % 2, er, ls] = g * w

    up(0)
    for pc in range(npieces):
        if pc + 1 < npieces:
            up(pc + 1)
        if pc >= 1:
            down(pc - 1)
        gate(pc)
    down(npieces - 1)


def _peer_ffn(h_t, layer, wq_t, keys, u_all, vt_bf):
    d, t = h_t.shape
    nq = wq_t.shape[1]
    tl = min(256, t)
    gate = pl.BlockSpec((PEER_HEADS, PEER_NKEYS, tl), lambda i: (0, 0, i))
    pair = pl.BlockSpec((PEER_HEADS, PEER_NKEYS // 2, tl), lambda i: (0, 0, i))
    gshape = jax.ShapeDtypeStruct((PEER_HEADS, PEER_NKEYS, t), F32)
    pshape = jax.ShapeDtypeStruct((PEER_HEADS, PEER_NKEYS // 2, t), jnp.uint32)
    a, cnt, b, rank = pl.pallas_call(
        _peer_route_kernel,
        grid=(t // tl,),
        in_specs=[pl.BlockSpec((d, tl), lambda i: (0, i)),
                  pl.BlockSpec((None, nq, d), lambda i: (layer, 0, 0)),
                  pl.BlockSpec((PEER_HEADS, 2, PEER_NKEYS, PEER_NKEYS), lambda i: (0, 0, 0, 0))],
        out_specs=[gate, gate, pair, pair],
        out_shape=[gshape, gshape, pshape, pshape],
        scratch_shapes=[pltpu.VMEM((nq, tl), F32)],
        compiler_params=_params("parallel"),
        name="peer_route",
    )(h_t, wq_t, keys)

    tl, te = min(1024, t), PEER_TE
    once = pl.Buffered(1)
    gate = pl.BlockSpec((PEER_HEADS, PEER_NKEYS, tl), lambda i, e: (0, 0, i), pipeline_mode=once)
    pair = pl.BlockSpec((PEER_HEADS, PEER_NKEYS // 2, tl), lambda i, e: (0, 0, i), pipeline_mode=once)
    return pl.pallas_call(
        functools.partial(_peer_dense_kernel, tl=tl),
        grid=(t // tl, PEER_EXPERTS // te),
        in_specs=[pl.BlockSpec((d, tl), lambda i, e: (0, i), pipeline_mode=once),
                  pl.BlockSpec((None, te, d), lambda i, e: (layer, e, 0)),
                  pl.BlockSpec((None, None, d, te), lambda i, e: (layer, e, 0, 0)),
                  gate, gate, pair, pair],
        out_specs=pl.BlockSpec((d, tl), lambda i, e: (0, i), pipeline_mode=once),
        out_shape=jax.ShapeDtypeStruct((d, t), F32),
        scratch_shapes=[pltpu.VMEM((2, PEER_PIECE * PEER_NKEYS, tl), F32),
                        pltpu.VMEM((2, PEER_PIECE * PEER_NKEYS, tl), BF16)],
        compiler_params=_params("parallel", "arbitrary"),
        name="peer_dense",
    )(h_t, u_all, vt_bf, a, cnt, b, rank)


def kernel(x, c, w_ada, b_ada, w_in, ssm_lam_re, ssm_lam_im, ssm_log_step, ssm_b_re, ssm_b_im, ssm_c_re, ssm_c_im, ssm_d, ssm_w_glu, dn_conv_w, dn_a_log, dn_dt_bias, dn_norm_w, w_out, ln1_g, ln1_b, peer_w_query, peer_sub_keys, peer_u, peer_v, ln2_g, ln2_b):
    bsz, seq, d = x.shape
    depth = w_ada.shape[0]
    t = bsz * seq
    alpha = (2.0 * depth) ** 0.25

    mod = _ada_mod(c, w_ada, b_ada)[:, :bsz]
    w_in_bf = w_in.astype(BF16)
    w_gate = jnp.zeros((depth, d, LANES), BF16).at[:, :, :2 * N_HEADS].set(w_in_bf[:, :, N_MAIN:])
    wq_t = peer_w_query.transpose(0, 2, 1).astype(BF16)
    vt_bf = peer_v.reshape(depth, -1, PEER_TE, d).transpose(0, 1, 3, 2).astype(BF16)
    xf = x.reshape(t, d)
    for l in range(depth):
        sh1, sc1, gt1, sh2, sc2, gt2 = [m.reshape(bsz, 1, d) for m in jnp.split(mod[l], 6, axis=-1)]

        proj, gates_raw = _in_proj(xf, sh1, sc1, w_in_bf, w_gate, l, seq)

        prep = _s5_prep(ssm_lam_re[l], ssm_lam_im[l], ssm_log_step[l], ssm_b_re[l], ssm_b_im[l],
                        ssm_c_re[l], ssm_c_im[l], ssm_d[l])
        y_ssm = _s5_mixer(proj, bsz, seq, prep, ssm_w_glu[l].astype(BF16))
        y_dn = _dn_mixer(proj, gates_raw, bsz, seq, dn_conv_w[l], dn_a_log[l], dn_dt_bias[l],
                         dn_norm_w[l])
        xf, hffn_t = _out_ln(y_ssm, y_dn, w_out[l].astype(BF16), xf, gt1, ln1_g[l], ln1_b[l],
                             sh2, sc2, seq, alpha)

        y_t = _peer_ffn(hffn_t, l, wq_t, peer_sub_keys[l], peer_u, vt_bf)
        xf = _res_ln(xf, y_t, gt2, ln2_g[l], ln2_b[l], seq, alpha)
    return xf.reshape(bsz, seq, d)
```
